```python
import math
import jax, jax.numpy as jnp
from jax import lax
import numpy as np

D_MODEL = 1024
BATCH = 8
SEQ = 4096
DEPTH = 1
DEC_BATCH = 128
DEC_SEQ = 8
PAST_LEN = 16384
PAGE_SIZE = 128

HEAD_DIM = 64
A_HEADS = 16
A_KV_HEADS = 4
A_GROUP = A_HEADS // A_KV_HEADS
A_WIDTH = A_HEADS * HEAD_DIM
KV_WIDTH = A_KV_HEADS * HEAD_DIM
WINDOW = 128
BLOCK = WINDOW
ATTN_SCALE = HEAD_DIM ** -0.5
N_BUCKETS = 32
MAX_DISTANCE = 128
B_HEADS = 16
B_WIDTH = B_HEADS * HEAD_DIM
DECAY_LORA = 64
A_LORA = 64
SHIFT_W = 3 * B_WIDTH + DECAY_LORA + A_LORA
GN_EPS = 64e-5
NORM_EPS = 1e-6
IN_SIZES = (A_WIDTH, KV_WIDTH, KV_WIDTH, A_WIDTH, SHIFT_W, B_WIDTH, D_MODEL, D_MODEL)
IN_COLS = 2 * A_WIDTH + 2 * KV_WIDTH + SHIFT_W + B_WIDTH + 2 * D_MODEL
SHIFT_SIZES = (B_WIDTH, B_WIDTH, B_WIDTH, DECAY_LORA, A_LORA)

kernel_name = "hybrid_swa_sink_rwkv7_gated_merge_step"


def _split(p, sizes):
    out, start = [], 0
    for n in sizes:
        out.append(p[..., start:start + n])
        start += n
    return out


def rmsnorm(x, g):
    xf = x.astype(jnp.float32)
    y = xf * lax.rsqrt(jnp.mean(xf * xf, axis=-1, keepdims=True) + NORM_EPS)
    return (y * g.astype(jnp.float32)).astype(x.dtype)


def t5_bucket(dist):
    max_exact = N_BUCKETS // 2
    d = jnp.maximum(dist, 1).astype(jnp.float32)
    large = max_exact + (jnp.log(d / max_exact) / math.log(MAX_DISTANCE / max_exact)
                         * (N_BUCKETS - max_exact)).astype(jnp.int32)
    large = jnp.minimum(large, N_BUCKETS - 1)
    return jnp.where(dist < max_exact, dist, large)


def window_bias(rel_bias, q_pos, k_pos):
    dist = q_pos[:, None] - k_pos[None, :]
    valid = (dist >= 0) & (dist <= WINDOW)
    bias = rel_bias.astype(jnp.float32)[t5_bucket(jnp.maximum(dist, 0))]
    bias = jnp.moveaxis(bias, -1, 0).reshape(A_KV_HEADS, A_GROUP, q_pos.shape[0], k_pos.shape[0])
    return bias, valid


def sink_softmax(s, sinks):
    sk = sinks.astype(jnp.float32)[..., None]
    m = jnp.maximum(jnp.max(s, axis=-1), sk)
    p = jnp.exp(s - m[..., None])
    denom = jnp.sum(p, axis=-1) + jnp.exp(sk - m)
    return p / denom[..., None]


def swa_banded(q, k, v, rel_bias, sinks):
    Bn, T = q.shape[0], q.shape[1]
    nb = T // BLOCK
    qb = q.reshape(Bn, nb, BLOCK, A_KV_HEADS, A_GROUP, HEAD_DIM)
    kb = k.reshape(Bn, nb, BLOCK, A_KV_HEADS, HEAD_DIM)
    vb = v.reshape(Bn, nb, BLOCK, A_KV_HEADS, HEAD_DIM)
    kband = jnp.concatenate([jnp.concatenate([jnp.zeros_like(kb[:, :1]), kb[:, :-1]], 1), kb], 2)
    vband = jnp.concatenate([jnp.concatenate([jnp.zeros_like(vb[:, :1]), vb[:, :-1]], 1), vb], 2)
    k_pos = jnp.arange(2 * BLOCK)
    bias, valid = window_bias(rel_bias, BLOCK + jnp.arange(BLOCK), k_pos)
    no_prev = (jnp.arange(nb)[:, None, None] == 0) & (k_pos[None, None, :] < BLOCK)
    valid_b = valid[None] & ~no_prev

    def one_sequence(args):
        qs, ks, vs = args
        s = jnp.einsum('nqhgd,nkhd->nhgqk', qs, ks, preferred_element_type=jnp.float32) * ATTN_SCALE + bias
        s = jnp.where(valid_b[:, None, None], s, -jnp.inf)
        p = sink_softmax(s, sinks)
        return jnp.einsum('nhgqk,nkhd->nqhgd', p.astype(vs.dtype), vs)

    o = lax.map(one_sequence, (qb, kband, vband))
    return o.reshape(Bn, T, A_WIDTH)


def swa_with_buffer(q, k, v, k_buf, v_buf, rel_bias, sinks):
    Bn, S = q.shape[0], q.shape[1]
    L = k_buf.shape[1]
    kc = jnp.concatenate([k_buf.astype(k.dtype), k], 1)
    vc = jnp.concatenate([v_buf.astype(v.dtype), v], 1)
    bias, valid = window_bias(rel_bias, L + jnp.arange(S), jnp.arange(L + S))
    s = jnp.einsum('bqhgd,bkhd->bhgqk', q, kc, preferred_element_type=jnp.float32) * ATTN_SCALE + bias
    s = jnp.where(valid, s, -jnp.inf)
    p = sink_softmax(s, sinks)
    o = jnp.einsum('bhgqk,bkhd->bqhgd', p.astype(vc.dtype), vc)
    return o.reshape(Bn, S, A_WIDTH), kc[:, S:], vc[:, S:]


def wkv_scan(r, w, k, v, a, b, S0):
    xs = tuple(jnp.moveaxis(t, 1, 0) for t in (r, w, k, v, a, b))

    def step(S, inp):
        rt, wt, kt, vt, at, bt = inp
        sa = jnp.einsum('bhvk,bhk->bhv', S, at)
        S = S * wt[:, :, None, :] + sa[..., None] * bt[:, :, None, :] + vt[..., None] * kt[:, :, None, :]
        return S, jnp.einsum('bhvk,bhk->bhv', S, rt)

    S_T, ys = lax.scan(step, S0.astype(jnp.float32), xs)
    return jnp.moveaxis(ys, 0, 1), S_T


def rwkv_time_mix(ps, shift0, wkv0, mu, w0, w2, a0, a2, k_k, k_a, r_k, lnx_g, lnx_b):
    Bn, T, _ = ps.shape
    prev = jnp.concatenate([shift0[:, None].astype(ps.dtype), ps[:, :-1]], 1)
    z = ps + (prev - ps) * mu
    r, k, v, wl, al = _split(z, SHIFT_SIZES)
    w_raw = -jax.nn.softplus(-(w0 + jnp.tanh(wl) @ w2)) - 0.5
    decay = jnp.exp(-jnp.exp(w_raw.astype(jnp.float32)))
    a = jax.nn.sigmoid((a0 + al @ a2).astype(jnp.float32))

    def heads(t):
        return t.reshape(Bn, T, B_HEADS, HEAD_DIM).astype(jnp.float32)

    r, k, v, a, decay = heads(r), heads(k), heads(v), heads(a), heads(decay)
    kk = k * k_k.reshape(B_HEADS, HEAD_DIM).astype(jnp.float32)
    kk = kk / jnp.maximum(jnp.sqrt(jnp.sum(kk * kk, axis=-1, keepdims=True)), 1e-12)
    k = k * (1.0 + (a - 1.0) * k_a.reshape(B_HEADS, HEAD_DIM).astype(jnp.float32))
    y, wkv_T = wkv_scan(r, decay, k, v, -kk, kk * a, wkv0)
    mean = jnp.mean(y, axis=-1, keepdims=True)
    var = jnp.mean(jnp.square(y - mean), axis=-1, keepdims=True)
    y = (y - mean) * lax.rsqrt(var + GN_EPS) * lnx_g.reshape(B_HEADS, HEAD_DIM).astype(jnp.float32) \
        + lnx_b.reshape(B_HEADS, HEAD_DIM).astype(jnp.float32)
    y = y + jnp.sum(r * k * r_k.astype(jnp.float32), axis=-1, keepdims=True) * v
    return y.reshape(Bn, T, B_WIDTH).astype(ps.dtype), wkv_T


def mixer_layer(x, k_buf, v_buf, wkv0, shift0, rel_bias, norm_g, w_in, sinks, mu, w0, w2, a0, a2,
                k_k, k_a, r_k, lnx_g, lnx_b, w_out_a, w_out_b, w_o):
    Bn, T, _ = x.shape
    h = rmsnorm(x, norm_g)
    p = jnp.einsum('btd,dc->btc', h, w_in)
    q, k, v, ga, ps, gb, ma, mb = _split(p, IN_SIZES)
    q = q.reshape(Bn, T, A_KV_HEADS, A_GROUP, HEAD_DIM)
    k = k.reshape(Bn, T, A_KV_HEADS, HEAD_DIM)
    v = v.reshape(Bn, T, A_KV_HEADS, HEAD_DIM)
    sinks_g = sinks.reshape(A_KV_HEADS, A_GROUP)
    if k_buf is None:
        o = swa_banded(q, k, v, rel_bias, sinks_g)
        keep = min(WINDOW, T)
        new_k, new_v = k[:, T - keep:], v[:, T - keep:]
        shift0 = jnp.zeros((Bn, SHIFT_W), ps.dtype)
        wkv0 = jnp.zeros((Bn, B_HEADS, HEAD_DIM, HEAD_DIM), jnp.float32)
    else:
        o, new_k, new_v = swa_with_buffer(q, k, v, k_buf, v_buf, rel_bias, sinks_g)
    ya = jnp.einsum('btc,cd->btd', o * jax.nn.silu(ga), w_out_a)
    yb_raw, wkv_T = rwkv_time_mix(ps, shift0, wkv0, mu, w0, w2, a0, a2, k_k, k_a, r_k, lnx_g, lnx_b)
    yb = jnp.einsum('btc,cd->btd', yb_raw * jax.nn.silu(gb), w_out_b)
    merged = jax.nn.sigmoid(ma) * ya + jax.nn.sigmoid(mb) * yb
    out = x + jnp.einsum('btd,de->bte', merged, w_o)
    return out, new_k, new_v, wkv_T, ps[:, -1]


def setup_inputs(seed: int = 0) -> dict:
    key = jax.random.key(seed)
    ks = jax.random.split(key, 24)

    def nrm(k, shape, scale):
        return jax.random.normal(k, shape, jnp.float32) * scale

    win = min(WINDOW, PAST_LEN)
    return {
        "x_prompt": nrm(ks[0], (BATCH, SEQ, D_MODEL), 1.0),
        "x_sample": nrm(ks[1], (DEC_BATCH, DEC_SEQ, D_MODEL), 1.0),
        "cache_k_win": nrm(ks[2], (DEPTH, DEC_BATCH, win, A_KV_HEADS, HEAD_DIM), 1.0),
        "cache_v_win": nrm(ks[3], (DEPTH, DEC_BATCH, win, A_KV_HEADS, HEAD_DIM), 1.0),
        "state_wkv": nrm(ks[4], (DEPTH, DEC_BATCH, B_HEADS, HEAD_DIM, HEAD_DIM), 0.3),
        "state_shift": nrm(ks[5], (DEPTH, DEC_BATCH, SHIFT_W), 1.0),
        "rel_bias": nrm(ks[6], (N_BUCKETS, A_HEADS), 0.3),
        "norm_g": 1.0 + nrm(ks[7], (DEPTH, D_MODEL), 0.02),
        "w_in": nrm(ks[8], (DEPTH, D_MODEL, IN_COLS), D_MODEL ** -0.5),
        "attn_sinks": nrm(ks[9], (DEPTH, A_HEADS), 0.5),
        "shift_mu": jax.random.uniform(ks[10], (DEPTH, SHIFT_W), jnp.float32),
        "rwkv_w0": jax.random.uniform(ks[11], (DEPTH, B_WIDTH), jnp.float32, -4.0, 1.0),
        "rwkv_w2": nrm(ks[12], (DEPTH, DECAY_LORA, B_WIDTH), 0.1 * DECAY_LORA ** -0.5),
        "rwkv_a0": nrm(ks[13], (DEPTH, B_WIDTH), 0.3),
        "rwkv_a2": nrm(ks[14], (DEPTH, A_LORA, B_WIDTH), 0.1 * A_LORA ** -0.5),
        "rwkv_k_k": 0.85 + nrm(ks[15], (DEPTH, B_WIDTH), 0.05),
        "rwkv_k_a": 1.0 + nrm(ks[16], (DEPTH, B_WIDTH), 0.05),
        "rwkv_r_k": nrm(ks[17], (DEPTH, B_HEADS, HEAD_DIM), 0.1),
        "lnx_g": 1.0 + nrm(ks[18], (DEPTH, B_WIDTH), 0.02),
        "lnx_b": nrm(ks[19], (DEPTH, B_WIDTH), 0.02),
        "w_out_a": nrm(ks[20], (DEPTH, A_WIDTH, D_MODEL), A_WIDTH ** -0.5),
        "w_out_b": nrm(ks[21], (DEPTH, B_WIDTH, D_MODEL), B_WIDTH ** -0.5),
        "w_o": nrm(ks[22], (DEPTH, D_MODEL, D_MODEL), D_MODEL ** -0.5),
        "final_g": 1.0 + nrm(ks[23], (D_MODEL,), 0.02),
    }


def reference(x_prompt, x_sample, cache_k_win, cache_v_win, state_wkv, state_shift, rel_bias, norm_g,
              w_in, attn_sinks, shift_mu, rwkv_w0, rwkv_w2, rwkv_a0, rwkv_a2, rwkv_k_k, rwkv_k_a,
              rwkv_r_k, lnx_g, lnx_b, w_out_a, w_out_b, w_o, final_g):
    hp, hs = x_prompt, x_sample
    pk, pv, pw, psh, sk, sv, sw, ssh = [], [], [], [], [], [], [], []
    for l in range(DEPTH):
        lw = (norm_g[l], w_in[l], attn_sinks[l], shift_mu[l], rwkv_w0[l], rwkv_w2[l], rwkv_a0[l],
              rwkv_a2[l], rwkv_k_k[l], rwkv_k_a[l], rwkv_r_k[l], lnx_g[l], lnx_b[l],
              w_out_a[l], w_out_b[l], w_o[l])
        hp, k1, v1, s1, t1 = mixer_layer(hp, None, None, None, None, rel_bias, *lw)
        hs, k2, v2, s2, t2 = mixer_layer(hs, cache_k_win[l], cache_v_win[l], state_wkv[l],
                                         state_shift[l], rel_bias, *lw)
        pk.append(k1); pv.append(v1); pw.append(s1); psh.append(t1)
        sk.append(k2); sv.append(v2); sw.append(s2); ssh.append(t2)
    y_prompt = rmsnorm(hp, final_g)
    y_sample = rmsnorm(hs, final_g)
    prompt_k_win, prompt_v_win = jnp.stack(pk), jnp.stack(pv)
    prompt_wkv, prompt_shift = jnp.stack(pw), jnp.stack(psh)
    sample_k_win, sample_v_win = jnp.stack(sk), jnp.stack(sv)
    sample_wkv, sample_shift = jnp.stack(sw), jnp.stack(ssh)
    return (y_prompt, y_sample, prompt_k_win, prompt_v_win, prompt_wkv, prompt_shift,
            sample_k_win, sample_v_win, sample_wkv, sample_shift)
```

```python
import functools
import math

import numpy as np
import jax
import jax.numpy as jnp
from jax import lax
from jax.experimental import pallas as pl
from jax.experimental.pallas import tpu as pltpu

D_MODEL = 1024
HEAD_DIM = 64
A_HEADS = 16
A_KV_HEADS = 4
A_WIDTH = A_HEADS * HEAD_DIM
KV_WIDTH = A_KV_HEADS * HEAD_DIM
WINDOW = 128
ATTN_SCALE = HEAD_DIM ** -0.5
N_BUCKETS = 32
MAX_DISTANCE = 128
B_HEADS = 16
B_WIDTH = B_HEADS * HEAD_DIM
DECAY_LORA = 64
A_LORA = 64
SHIFT_W = 3 * B_WIDTH + DECAY_LORA + A_LORA
GN_EPS = 64e-5
NORM_EPS = 1e-6

ATTN_COLS = 2 * A_WIDTH + 2 * KV_WIDTH
RWKV_COLS = SHIFT_W + B_WIDTH
MERGE_COLS = 2 * D_MODEL

ROWS = 128
LANES = 128
N_PAIRS = B_HEADS // 2
NEG = -1e30
VMEM_LIMIT = 56 * 1024 * 1024

F32 = jnp.float32
BF16 = jnp.bfloat16


def _bucket_ranges():
    d = np.arange(0, WINDOW + 1)
    max_exact = N_BUCKETS // 2
    df = np.maximum(d, 1).astype(np.float32)
    large = max_exact + (np.log(df / np.float32(max_exact)) / np.float32(math.log(MAX_DISTANCE / max_exact))
                         * np.float32(N_BUCKETS - max_exact)).astype(np.int32)
    large = np.minimum(large, N_BUCKETS - 1)
    bucket = np.where(d < max_exact, d, large)
    out = []
    for b in range(N_BUCKETS):
        idx = np.nonzero(bucket == b)[0]
        if idx.size:
            assert idx[-1] - idx[0] + 1 == idx.size
            out.append((b, int(idx[0]), int(idx[-1])))
    return tuple(out)


_BUCKET_RANGES = _bucket_ranges()


def _bf(x):
    return x.astype(BF16)


def _mm(a, b):
    return jnp.dot(_bf(a), _bf(b), preferred_element_type=F32)


def _mm_nt(a, b):
    return lax.dot_general(_bf(a), _bf(b), (((1,), (1,)), ((), ())), preferred_element_type=F32)


def _rms(x, g):
    ms = jnp.mean(x * x, axis=-1, keepdims=True)
    return x * lax.rsqrt(ms + NORM_EPS) * g


def _sigmoid(x):
    return 1.0 / (1.0 + jnp.exp(-x))


def _silu(x):
    return x * _sigmoid(x)


def _softplus(x):
    return jnp.maximum(x, 0.0) + jnp.log(1.0 + jnp.exp(-jnp.abs(x)))


def _init_bias(relb_ref, bprev_ref, bcur_ref):
    qi = lax.broadcasted_iota(jnp.int32, (WINDOW, WINDOW), 0)
    kj = lax.broadcasted_iota(jnp.int32, (WINDOW, WINDOW), 1)
    delta = qi - kj

    def body(h, carry):
        bp = jnp.full((WINDOW, WINDOW), NEG, F32)
        bc = jnp.full((WINDOW, WINDOW), NEG, F32)
        for (b, lo, hi) in _BUCKET_RANGES:
            val = relb_ref[b, h]
            bp = jnp.where((delta >= lo - WINDOW) & (delta <= hi - WINDOW), val, bp)
            bc = jnp.where((delta >= lo) & (delta <= hi), val, bc)
        bprev_ref[h] = bp
        bcur_ref[h] = bc
        return carry

    lax.fori_loop(0, A_HEADS, body, 0)


def _attn_core(q, kp, vp, kc, vc, bprev_ref, bcur_ref, sinks_ref, tq, first):
    group = A_HEADS // A_KV_HEADS
    lo_half = lax.broadcasted_iota(jnp.int32, (1, LANES), 1) < HEAD_DIM
    rowi = lax.broadcasted_iota(jnp.int32, (group * tq, 1), 0)
    pieces = [None] * A_HEADS
    for i in range(KV_WIDTH // LANES):
        sl = slice(LANES * i, LANES * (i + 1))
        kpi, vpi, kci, vci = _bf(kp[:, sl]), _bf(vp[:, sl]), _bf(kc[:, sl]), _bf(vc[:, sl])
        for c in range(2):
            kvh = 2 * i + c
            rows = []
            for g in range(group):
                hq = group * kvh + g
                qs = q[:, LANES * (hq // 2):LANES * (hq // 2 + 1)]
                qm = jnp.where(lo_half if hq % 2 == 0 else jnp.logical_not(lo_half), qs, 0.0)
                if hq % 2 != c:
                    qm = pltpu.roll(qm, HEAD_DIM, 1)
                rows.append(qm)
            lhs = _bf(jnp.concatenate(rows, axis=0))
            bp = bprev_ref[group * kvh:group * (kvh + 1), 0:tq, :].reshape(group * tq, WINDOW)
            bc = bcur_ref[group * kvh:group * (kvh + 1), 0:tq, :].reshape(group * tq, WINDOW)
            sp = _mm_nt(lhs, kpi) + bp
            if first is not None:
                sp = jnp.where(first, NEG, sp)
            sc = _mm_nt(lhs, kci) + bc
            sink = sinks_ref[group * kvh + group - 1]
            for g in range(group - 2, -1, -1):
                sink = jnp.where(rowi < (g + 1) * tq, sinks_ref[group * kvh + g], sink)
            m = jnp.maximum(jnp.maximum(jnp.max(sp, axis=-1, keepdims=True),
                                        jnp.max(sc, axis=-1, keepdims=True)), sink)
            pp = jnp.exp(sp - m)
            pc = jnp.exp(sc - m)
            den = jnp.sum(pp, axis=-1, keepdims=True) + jnp.sum(pc, axis=-1, keepdims=True) + jnp.exp(sink - m)
            o = (_mm(pp, vpi) + _mm(pc, vci)) / den
            for g in range(group):
                hq = group * kvh + g
                og = o[g * tq:(g + 1) * tq]
                if hq % 2 != c:
                    og = pltpu.roll(og, HEAD_DIM, 1)
                pieces[hq] = og
    slabs = [jnp.where(lo_half, pieces[2 * s], pieces[2 * s + 1]) for s in range(A_HEADS // 2)]
    return jnp.concatenate(slabs, axis=1)


def _attn_project(x, g_ref, wa_ref):
    h = _rms(x, g_ref[...])
    proj = _mm(h, wa_ref[...])
    q = proj[:, :A_WIDTH] * ATTN_SCALE
    k = proj[:, A_WIDTH:A_WIDTH + KV_WIDTH]
    v = proj[:, A_WIDTH + KV_WIDTH:A_WIDTH + 2 * KV_WIDTH]
    ga = proj[:, A_WIDTH + 2 * KV_WIDTH:]
    return q, k, v, ga


def _attn_prompt_kernel(x_ref, g_ref, wa_ref, woa_ref, relb_ref, sinks_ref,
                        ya_ref, kw_ref, vw_ref, kprev, vprev, bprev, bcur):
    b = pl.program_id(0)
    n = pl.program_id(1)

    @pl.when((b == 0) & (n == 0))
    def _():
        _init_bias(relb_ref, bprev, bcur)

    @pl.when(n == 0)
    def _():
        kprev[...] = jnp.zeros_like(kprev)
        vprev[...] = jnp.zeros_like(vprev)

    q, k, v, ga = _attn_project(x_ref[0], g_ref, wa_ref)
    o = _attn_core(q, kprev[...], vprev[...], k, v, bprev, bcur, sinks_ref, ROWS, n == 0)
    kprev[...] = k
    vprev[...] = v
    kw_ref[0] = k
    vw_ref[0] = v
    ya_ref[0] = _mm(o * _silu(ga), woa_ref[...])


def _attn_sample_kernel(x_ref, g_ref, wa_ref, woa_ref, relb_ref, sinks_ref, kc_ref, vc_ref,
                        ya_ref, ko_ref, vo_ref, qbuf, kbuf, vbuf, obuf, bprev, bcur, *, seq):
    @pl.when(pl.program_id(0) == 0)
    def _():
        _init_bias(relb_ref, bprev, bcur)

    q, k, v, ga = _attn_project(x_ref[...], g_ref, wa_ref)
    qbuf[...] = q
    kbuf[...] = k
    vbuf[...] = v
    pad = jnp.zeros((WINDOW - seq, KV_WIDTH), F32)

    def body(j, carry):
        rows = pl.ds(pl.multiple_of(j * seq, seq), seq)
        kn = kbuf[rows, :]
        vn = vbuf[rows, :]
        kcj = kc_ref[j]
        vcj = vc_ref[j]
        o = _attn_core(qbuf[rows, :], kcj, vcj, jnp.concatenate([kn, pad], axis=0),
                       jnp.concatenate([vn, pad], axis=0), bprev, bcur, sinks_ref, seq, None)
        obuf[rows, :] = o
        ko_ref[j, pl.ds(0, WINDOW - seq), :] = kcj[seq:, :]
        ko_ref[j, pl.ds(WINDOW - seq, seq), :] = kn
        vo_ref[j, pl.ds(0, WINDOW - seq), :] = vcj[seq:, :]
        vo_ref[j, pl.ds(WINDOW - seq, seq), :] = vn
        return carry

    lax.fori_loop(0, ROWS // seq, body, 0)
    ya_ref[...] = _mm(obuf[...] * _silu(ga), woa_ref[...])


def _head_sums(x, ones_blocks):
    outs = []
    width = ones_blocks.shape[0]
    for i in range(x.shape[1] // width):
        xs = x[:, width * i:width * (i + 1)]
        hi = _bf(xs)
        lo = _bf(xs - hi.astype(F32))
        outs.append(jnp.dot(hi, ones_blocks, preferred_element_type=F32)
                    + jnp.dot(lo, ones_blocks, preferred_element_type=F32))
    return jnp.concatenate(outs, axis=1)


def _seg_cumsum(x, seq):
    pos = lax.broadcasted_iota(jnp.int32, (x.shape[0], 1), 0) & (seq - 1)
    s = 1
    while s < seq:
        x = x + jnp.where(pos >= s, pltpu.roll(x, s, 0), 0.0)
        s *= 2
    return x


def _neumann(p, seq):
    n = p.shape[0]
    ri = lax.broadcasted_iota(jnp.int32, (n, n), 0)
    ci = lax.broadcasted_iota(jnp.int32, (n, n), 1)
    x = jnp.where(ri == ci, 1.0, 0.0) + p
    if seq <= 2:
        return x
    pw = _mm(p, p)
    span = 2
    while 2 * span < seq:
        res = _mm(pw, jnp.concatenate([pw, x], axis=1))
        x = x + res[:, n:]
        pw = res[:, :n]
        span *= 2
    return x + _mm(pw, x)


def _rwkv_kernel(*refs, seq, prompt):
    nseq = ROWS // seq
    if prompt:
        (x_ref, g_ref, wr_ref, wob_ref, mu_ref, w0_ref, lora_ref, a0_ref, kk_ref, ka_ref, rk_ref,
         lng_ref, lnb_ref,
         yb_ref, so_ref, sh_ref,
         at_ref, rt_ref, bt_ref, kt_ref, bh_ref, kh_ref, v_ref, gt_ref, r_ref, kp_ref, gb_ref,
         x0_ref, y0_ref, u_ref, y_ref, av_ref, t_ref, arbk_ref, s_ref, carry_ref) = refs
        n = pl.program_id(1)
        x = x_ref[0]
    else:
        (x_ref, g_ref, wr_ref, wob_ref, mu_ref, w0_ref, lora_ref, a0_ref, kk_ref, ka_ref, rk_ref,
         lng_ref, lnb_ref, si_ref, shin_ref,
         yb_ref, so_ref, ps_ref,
         at_ref, rt_ref, bt_ref, kt_ref, bh_ref, kh_ref, v_ref, gt_ref, r_ref, kp_ref, gb_ref,
         x0_ref, y0_ref, u_ref, y_ref, av_ref, t_ref, arbk_ref, uvt_ref, bkh_ref) = refs
        x = x_ref[...]

    h = _rms(x, g_ref[...])
    proj = _mm(h, wr_ref[...])
    ps = proj[:, :SHIFT_W]
    gb_ref[...] = proj[:, SHIFT_W:]
    pos = lax.broadcasted_iota(jnp.int32, (ROWS, 1), 0) & (seq - 1)
    if prompt:
        @pl.when(n == 0)
        def _():
            carry_ref[...] = jnp.zeros_like(carry_ref)
            s_ref[...] = jnp.zeros_like(s_ref)
        first_rows = carry_ref[...]
    else:
        first_rows = shin_ref[...]
    prev = jnp.where(pos == 0, first_rows, pltpu.roll(ps, 1, 0))
    if prompt:
        carry_ref[...] = ps[ROWS - 1:ROWS, :]
        sh_ref[0] = ps[ROWS - 1:ROWS, :]
    else:
        ps_ref[...] = ps
    z = ps + (prev - ps) * mu_ref[...]
    r = z[:, :B_WIDTH]
    k = z[:, B_WIDTH:2 * B_WIDTH]
    v = z[:, 2 * B_WIDTH:3 * B_WIDTH]
    zl = z[:, 3 * B_WIDTH:]
    lo_half = lax.broadcasted_iota(jnp.int32, (1, LANES), 1) < HEAD_DIM
    lora = _mm(jnp.where(lo_half, jnp.tanh(zl), zl), lora_ref[...])
    w_raw = -_softplus(-(w0_ref[...] + lora[:, :B_WIDTH])) - 0.5
    lw = -jnp.exp(w_raw)
    asig = _sigmoid(a0_ref[...] + lora[:, B_WIDTH:])
    ri = lax.broadcasted_iota(jnp.int32, (2 * LANES, 2 * LANES), 0)
    ci = lax.broadcasted_iota(jnp.int32, (2 * LANES, 2 * LANES), 1)
    ones_blocks = jnp.where((ri >> 6) == (ci >> 6), 1.0, 0.0).astype(BF16)
    kkr = k * kk_ref[...]
    kk = kkr / jnp.maximum(jnp.sqrt(_head_sums(kkr * kkr, ones_blocks)), 1e-12)
    kp = k * (1.0 + (asig - 1.0) * ka_ref[...])
    a_ = -kk
    b_ = kk * asig
    cum = _seg_cumsum(lw, seq)
    if nseq == 1:
        tot = jnp.broadcast_to(cum[ROWS - 1:ROWS, :], (ROWS, B_WIDTH))
    else:
        c3 = cum.reshape(nseq, seq, B_WIDTH)
        tot = jnp.broadcast_to(c3[:, seq - 1:seq, :], (nseq, seq, B_WIDTH)).reshape(ROWS, B_WIDTH)
    g_inv = jnp.exp(-cum)
    g_last = jnp.exp(tot - cum)
    at_ref[...] = a_ * jnp.exp(cum - lw)
    rt_ref[...] = r * jnp.exp(cum)
    bt_ref[...] = b_ * g_inv
    kt_ref[...] = kp * g_inv
    bh_ref[...] = b_ * g_last
    kh_ref[...] = kp * g_last
    gt_ref[...] = jnp.exp(tot)
    v_ref[...] = v
    r_ref[...] = r
    kp_ref[...] = kp

    rr = lax.broadcasted_iota(jnp.int32, (ROWS, ROWS), 0)
    cc = lax.broadcasted_iota(jnp.int32, (ROWS, ROWS), 1)
    shift = int(math.log2(seq))
    same = (rr >> shift) == (cc >> shift)
    strict = same & (rr > cc)
    incl = same & (rr >= cc)
    hi_half = jnp.logical_not(lo_half)
    bd = (rr < HEAD_DIM) == (cc < HEAD_DIM)

    def halves(t):
        return jnp.concatenate([jnp.where(lo_half, t, 0.0), jnp.where(hi_half, t, 0.0)], axis=0)

    for p in range(N_PAIRS):
        sl = slice(LANES * p, LANES * (p + 1))
        ar = jnp.concatenate([at_ref[:, sl], rt_ref[:, sl]], axis=0)
        bk = jnp.concatenate([bt_ref[:, sl], kt_ref[:, sl]], axis=0)
        gm = _mm_nt(halves(ar), bk)
        ts, aks, rbk = [], [], []
        for hh in range(2):
            base = 2 * ROWS * hh
            a_ab = jnp.where(strict, gm[base:base + ROWS, :ROWS], 0.0)
            a_ak = jnp.where(strict, gm[base:base + ROWS, ROWS:], 0.0)
            a_rb = jnp.where(incl, gm[base + ROWS:base + 2 * ROWS, :ROWS], 0.0)
            a_rk = jnp.where(incl, gm[base + ROWS:base + 2 * ROWS, ROWS:], 0.0)
            ts.append(_neumann(a_ab, seq))
            aks.append(a_ak)
            rbk += [a_rb, a_rk]
        t_ref[p] = _bf(jnp.concatenate(ts, axis=1))
        arbk_ref[p] = _bf(jnp.concatenate(rbk, axis=1))
        av_ref[:, sl] = _mm(jnp.concatenate(aks, axis=1), halves(v_ref[:, sl]))

    lo64 = lo_half

    def blockdiag(s_nat):
        return jnp.concatenate([jnp.where(lo64, s_nat, 0.0), jnp.where(lo64, 0.0, s_nat)], axis=0)

    if prompt:
        for p in range(N_PAIRS):
            sl = slice(LANES * p, LANES * (p + 1))
            s0 = s_ref[p]
            x0_ref[:, sl] = _mm_nt(at_ref[:, sl], s0)
            y0_ref[:, sl] = _mm_nt(rt_ref[:, sl], s0)
    else:
        def read_body(j, carry):
            rows = pl.ds(pl.multiple_of(j * seq, seq), seq)
            for p in range(N_PAIRS):
                sl = slice(LANES * p, LANES * (p + 1))
                s0 = blockdiag(si_ref[j, p])
                arj = jnp.concatenate([at_ref[rows, sl], rt_ref[rows, sl]], axis=0)
                xy = _mm_nt(arj, s0)
                x0_ref[rows, sl] = xy[:seq]
                y0_ref[rows, sl] = xy[seq:]
            return carry
        lax.fori_loop(0, nseq, read_body, 0)

    for p in range(N_PAIRS):
        sl = slice(LANES * p, LANES * (p + 1))
        vs = v_ref[:, sl]
        u = jnp.dot(t_ref[p], _bf(halves(x0_ref[:, sl] + av_ref[:, sl])), preferred_element_type=F32)
        u_ref[:, sl] = u
        rhs = jnp.concatenate([jnp.where(lo_half, u, 0.0), jnp.where(lo_half, vs, 0.0),
                               jnp.where(hi_half, u, 0.0), jnp.where(hi_half, vs, 0.0)], axis=0)
        y_ref[:, sl] = y0_ref[:, sl] + jnp.dot(arbk_ref[p], _bf(rhs), preferred_element_type=F32)

    if prompt:
        last = n == pl.num_programs(1) - 1
        for p in range(N_PAIRS):
            sl = slice(LANES * p, LANES * (p + 1))
            uv = jnp.concatenate([u_ref[:, sl], v_ref[:, sl]], axis=0)
            bkh = jnp.concatenate([bh_ref[:, sl], kh_ref[:, sl]], axis=0)
            s1 = s_ref[p] * gt_ref[0:1, sl] + jnp.where(bd, _mm(uv.T, bkh), 0.0)
            s_ref[p] = s1

            @pl.when(last)
            def _():
                so_ref[0, p] = jnp.where(lo64, s1[:HEAD_DIM], s1[HEAD_DIM:])
    else:
        colseq = (lax.broadcasted_iota(jnp.int32, (1, 2 * ROWS), 1) & (ROWS - 1)) >> shift
        for p in range(N_PAIRS):
            sl = slice(LANES * p, LANES * (p + 1))
            uv = jnp.concatenate([u_ref[:, sl], v_ref[:, sl]], axis=0)
            uvt_ref[p] = _bf(uv.T)
            bkh_ref[p] = _bf(jnp.concatenate([bh_ref[:, sl], kh_ref[:, sl]], axis=0))

        def upd_body(j, carry):
            row = pl.ds(pl.multiple_of(j * seq, seq), 1)
            for p in range(N_PAIRS):
                sl = slice(LANES * p, LANES * (p + 1))
                uvt = jnp.where(colseq == j, uvt_ref[p], jnp.zeros((), BF16))
                upd = jnp.dot(uvt, bkh_ref[p], preferred_element_type=F32)
                so_ref[j, p] = (si_ref[j, p] * gt_ref[row, sl]
                                + jnp.where(lo64, upd[:HEAD_DIM], upd[HEAD_DIM:]))
            return carry
        lax.fori_loop(0, nseq, upd_body, 0)

    y = y_ref[...]
    mean = _head_sums(y, ones_blocks) * (1.0 / HEAD_DIM)
    dlt = y - mean
    var = _head_sums(dlt * dlt, ones_blocks) * (1.0 / HEAD_DIM)
    yn = dlt * lax.rsqrt(var + GN_EPS) * lng_ref[...] + lnb_ref[...]
    bonus = _head_sums(r_ref[...] * kp_ref[...] * rk_ref[...], ones_blocks) * v_ref[...]
    yo = (yn + bonus) * _silu(gb_ref[...])
    if prompt:
        yb_ref[0] = _mm(yo, wob_ref[...])
    else:
        yb_ref[...] = _mm(yo, wob_ref[...])


def _merge_kernel(x_ref, ya_ref, yb_ref, g_ref, wm_ref, wo_ref, fg_ref, out_ref, *, final):
    x = x_ref[...]
    h = _rms(x, g_ref[...])
    mm = _mm(h, wm_ref[...])
    merged = _sigmoid(mm[:, :D_MODEL]) * ya_ref[...] + _sigmoid(mm[:, D_MODEL:]) * yb_ref[...]
    out = x + _mm(merged, wo_ref[...])
    out_ref[...] = _rms(out, fg_ref[...]) if final else out


def _const_spec(shape):
    nd = len(shape)
    return pl.BlockSpec(shape, lambda *_: (0,) * nd, pipeline_mode=pl.Buffered(1))


def _smem_spec():
    return pl.BlockSpec(memory_space=pltpu.SMEM)


def _params(n_axes):
    return pltpu.CompilerParams(dimension_semantics=("arbitrary",) * n_axes, vmem_limit_bytes=VMEM_LIMIT)


def _attention_prompt(x, g, wa, woa, rel_bias, sinks):
    bsz, t, _ = x.shape
    nb = t // ROWS
    tile = pl.BlockSpec((1, ROWS, D_MODEL), lambda b, n: (b, n, 0))
    win = pl.BlockSpec((1, WINDOW, KV_WIDTH), lambda b, n: (b, 0, 0))
    return pl.pallas_call(
        _attn_prompt_kernel,
        grid=(bsz, nb),
        in_specs=[tile, _const_spec((1, D_MODEL)), _const_spec((D_MODEL, ATTN_COLS)),
                  _const_spec((A_WIDTH, D_MODEL)), _smem_spec(), _smem_spec()],
        out_specs=[tile, win, win],
        out_shape=[jax.ShapeDtypeStruct((bsz, t, D_MODEL), F32),
                   jax.ShapeDtypeStruct((bsz, WINDOW, KV_WIDTH), F32),
                   jax.ShapeDtypeStruct((bsz, WINDOW, KV_WIDTH), F32)],
        scratch_shapes=[pltpu.VMEM((WINDOW, KV_WIDTH), F32), pltpu.VMEM((WINDOW, KV_WIDTH), F32),
                        pltpu.VMEM((A_HEADS, WINDOW, WINDOW), F32), pltpu.VMEM((A_HEADS, WINDOW, WINDOW), F32)],
        compiler_params=_params(2),
        name="attn_prompt",
    )(x, g, wa, woa, rel_bias, sinks)


def _attention_sample(x2d, seq, g, wa, woa, rel_bias, sinks, kc, vc):
    rows = x2d.shape[0]
    nseq = ROWS // seq
    tile = pl.BlockSpec((ROWS, D_MODEL), lambda i: (i, 0))
    cache = pl.BlockSpec((nseq, WINDOW, KV_WIDTH), lambda i: (i, 0, 0))
    return pl.pallas_call(
        functools.partial(_attn_sample_kernel, seq=seq),
        grid=(rows // ROWS,),
        in_specs=[tile, _const_spec((1, D_MODEL)), _const_spec((D_MODEL, ATTN_COLS)),
                  _const_spec((A_WIDTH, D_MODEL)), _smem_spec(), _smem_spec(), cache, cache],
        out_specs=[tile, cache, cache],
        out_shape=[jax.ShapeDtypeStruct((rows, D_MODEL), F32),
                   jax.ShapeDtypeStruct(kc.shape, F32), jax.ShapeDtypeStruct(vc.shape, F32)],
        scratch_shapes=[pltpu.VMEM((ROWS, A_WIDTH), F32), pltpu.VMEM((ROWS, KV_WIDTH), F32),
                        pltpu.VMEM((ROWS, KV_WIDTH), F32), pltpu.VMEM((ROWS, A_WIDTH), F32),
                        pltpu.VMEM((A_HEADS, WINDOW, WINDOW), F32), pltpu.VMEM((A_HEADS, WINDOW, WINDOW), F32)],
        compiler_params=_params(1),
        name="attn_sample",
    )(x2d, g, wa, woa, rel_bias, sinks, kc, vc)


def _rwkv_scratch():
    wide = pltpu.VMEM((ROWS, B_WIDTH), F32)
    return [wide] * 16 + [pltpu.VMEM((N_PAIRS, ROWS, 2 * ROWS), BF16), pltpu.VMEM((N_PAIRS, ROWS, 4 * ROWS), BF16)]


def _rwkv_weight_specs():
    row = _const_spec((1, B_WIDTH))
    return [_const_spec((1, D_MODEL)), _const_spec((D_MODEL, RWKV_COLS)), _const_spec((B_WIDTH, D_MODEL)),
            _const_spec((1, SHIFT_W)), row, _const_spec((LANES, 2 * B_WIDTH)), row, row, row, row, row, row]


def _rwkv_prompt(x, weights):
    bsz, t, _ = x.shape
    nb = t // ROWS
    tile = pl.BlockSpec((1, ROWS, D_MODEL), lambda b, n: (b, n, 0))
    return pl.pallas_call(
        functools.partial(_rwkv_kernel, seq=ROWS, prompt=True),
        grid=(bsz, nb),
        in_specs=[tile] + _rwkv_weight_specs(),
        out_specs=[tile,
                   pl.BlockSpec((1, N_PAIRS, HEAD_DIM, LANES), lambda b, n: (b, 0, 0, 0)),
                   pl.BlockSpec((1, 1, SHIFT_W), lambda b, n: (b, 0, 0))],
        out_shape=[jax.ShapeDtypeStruct((bsz, t, D_MODEL), F32),
                   jax.ShapeDtypeStruct((bsz, N_PAIRS, HEAD_DIM, LANES), F32),
                   jax.ShapeDtypeStruct((bsz, 1, SHIFT_W), F32)],
        scratch_shapes=_rwkv_scratch() + [pltpu.VMEM((N_PAIRS, LANES, LANES), F32),
                                          pltpu.VMEM((1, SHIFT_W), F32)],
        compiler_params=_params(2),
        name="rwkv_prompt",
    )(x, *weights)


def _rwkv_sample(x2d, seq, weights, s_nat, shift_rows):
    rows = x2d.shape[0]
    nseq = ROWS // seq
    tile = pl.BlockSpec((ROWS, D_MODEL), lambda i: (i, 0))
    state = pl.BlockSpec((nseq, N_PAIRS, HEAD_DIM, LANES), lambda i: (i, 0, 0, 0))
    shift = pl.BlockSpec((ROWS, SHIFT_W), lambda i: (i, 0))
    return pl.pallas_call(
        functools.partial(_rwkv_kernel, seq=seq, prompt=False),
        grid=(rows // ROWS,),
        in_specs=[tile] + _rwkv_weight_specs() + [state, shift],
        out_specs=[tile, state, shift],
        out_shape=[jax.ShapeDtypeStruct((rows, D_MODEL), F32),
                   jax.ShapeDtypeStruct(s_nat.shape, F32),
                   jax.ShapeDtypeStruct((rows, SHIFT_W), F32)],
        scratch_shapes=_rwkv_scratch() + [pltpu.VMEM((N_PAIRS, ROWS, 2 * ROWS), BF16),
                                          pltpu.VMEM((N_PAIRS, 2 * ROWS, LANES), BF16)],
        compiler_params=_params(1),
        name="rwkv_sample",
    )(x2d, *weights, s_nat, shift_rows)


def _merge(x2d, ya, yb, g, wm, wo, fg, final, name):
    rows = x2d.shape[0]
    tm = 256
    tile = pl.BlockSpec((tm, D_MODEL), lambda i: (i, 0))
    return pl.pallas_call(
        functools.partial(_merge_kernel, final=final),
        grid=(rows // tm,),
        in_specs=[tile, tile, tile, _const_spec((1, D_MODEL)), _const_spec((D_MODEL, MERGE_COLS)),
                  _const_spec((D_MODEL, D_MODEL)), _const_spec((1, D_MODEL))],
        out_specs=tile,
        out_shape=jax.ShapeDtypeStruct((rows, D_MODEL), F32),
        compiler_params=_params(1),
        name=name,
    )(x2d, ya, yb, g, wm, wo, fg)


def _pairs_from_heads(s):
    b = s.shape[0]
    return s.reshape(b, N_PAIRS, 2, HEAD_DIM, HEAD_DIM).transpose(0, 1, 3, 2, 4).reshape(b, N_PAIRS, HEAD_DIM, LANES)


def _heads_from_pairs(s):
    b = s.shape[0]
    return s.reshape(b, N_PAIRS, HEAD_DIM, 2, HEAD_DIM).transpose(0, 1, 3, 2, 4).reshape(b, B_HEADS, HEAD_DIM, HEAD_DIM)


def kernel(x_prompt, x_sample, cache_k_win, cache_v_win, state_wkv, state_shift, rel_bias, norm_g, w_in, attn_sinks, shift_mu, rwkv_w0, rwkv_w2, rwkv_a0, rwkv_a2, rwkv_k_k, rwkv_k_a, rwkv_r_k, lnx_g, lnx_b, w_out_a, w_out_b, w_o, final_g):
    depth = w_in.shape[0]
    bsz, t, _ = x_prompt.shape
    dbsz, dseq, _ = x_sample.shape
    hp = x_prompt
    hs = x_sample.reshape(dbsz * dseq, D_MODEL)
    fg = final_g.reshape(1, D_MODEL)
    outs = [[] for _ in range(8)]
    for l in range(depth):
        g = norm_g[l].reshape(1, D_MODEL)
        w = w_in[l]
        wa = _bf(w[:, :ATTN_COLS])
        wr = _bf(w[:, ATTN_COLS:ATTN_COLS + RWKV_COLS])
        wm = _bf(w[:, ATTN_COLS + RWKV_COLS:])
        woa, wob, wo = _bf(w_out_a[l]), _bf(w_out_b[l]), _bf(w_o[l])
        zeros = jnp.zeros((DECAY_LORA, B_WIDTH), F32)
        lora = _bf(jnp.concatenate([jnp.concatenate([rwkv_w2[l], zeros], axis=1),
                                    jnp.concatenate([zeros, rwkv_a2[l]], axis=1)], axis=0))
        rw = [g, wr, wob, shift_mu[l].reshape(1, SHIFT_W), rwkv_w0[l].reshape(1, B_WIDTH), lora,
              rwkv_a0[l].reshape(1, B_WIDTH), rwkv_k_k[l].reshape(1, B_WIDTH), rwkv_k_a[l].reshape(1, B_WIDTH),
              rwkv_r_k[l].reshape(1, B_WIDTH), lnx_g[l].reshape(1, B_WIDTH), lnx_b[l].reshape(1, B_WIDTH)]
        sinks = attn_sinks[l]

        ya_p, k1, v1 = _attention_prompt(hp, g, wa, woa, rel_bias, sinks)
        yb_p, s1, t1 = _rwkv_prompt(hp, rw)
        hp = _merge(hp.reshape(bsz * t, D_MODEL), ya_p.reshape(bsz * t, D_MODEL), yb_p.reshape(bsz * t, D_MODEL),
                    g, wm, wo, fg, l == depth - 1, "merge_prompt").reshape(bsz, t, D_MODEL)

        kc = cache_k_win[l].reshape(dbsz, WINDOW, KV_WIDTH)
        vc = cache_v_win[l].reshape(dbsz, WINDOW, KV_WIDTH)
        ya_s, k2, v2 = _attention_sample(hs, dseq, g, wa, woa, rel_bias, sinks, kc, vc)
        shift_rows = jnp.repeat(state_shift[l], dseq, axis=0)
        yb_s, s2, ps_s = _rwkv_sample(hs, dseq, rw, _pairs_from_heads(state_wkv[l]), shift_rows)
        hs = _merge(hs, ya_s, yb_s, g, wm, wo, fg, l == depth - 1, "merge_sample")

        outs[0].append(k1.reshape(bsz, WINDOW, A_KV_HEADS, HEAD_DIM))
        outs[1].append(v1.reshape(bsz, WINDOW, A_KV_HEADS, HEAD_DIM))
        outs[2].append(_heads_from_pairs(s1))
        outs[3].append(t1.reshape(bsz, SHIFT_W))
        outs[4].append(k2.reshape(dbsz, WINDOW, A_KV_HEADS, HEAD_DIM))
        outs[5].append(v2.reshape(dbsz, WINDOW, A_KV_HEADS, HEAD_DIM))
        outs[6].append(_heads_from_pairs(s2))
        outs[7].append(ps_s.reshape(dbsz, dseq, SHIFT_W)[:, -1])
    y_prompt = hp
    y_sample = hs.reshape(dbsz, dseq, D_MODEL)
    return (y_prompt, y_sample) + tuple(jnp.stack(o) for o in outs)
```

```python
import functools
import math

import numpy as np
import jax
import jax.numpy as jnp
from jax import lax
from jax.experimental import pallas as pl
from jax.experimental.pallas import tpu as pltpu

D_MODEL = 1024
HEAD_DIM = 64
A_HEADS = 16
A_KV_HEADS = 4
A_WIDTH = A_HEADS * HEAD_DIM
KV_WIDTH = A_KV_HEADS * HEAD_DIM
WINDOW = 128
ATTN_SCALE = HEAD_DIM ** -0.5
N_BUCKETS = 32
MAX_DISTANCE = 128
B_HEADS = 16
B_WIDTH = B_HEADS * HEAD_DIM
DECAY_LORA = 64
A_LORA = 64
SHIFT_W = 3 * B_WIDTH + DECAY_LORA + A_LORA
GN_EPS = 64e-5
NORM_EPS = 1e-6

ATTN_COLS = 2 * A_WIDTH + 2 * KV_WIDTH
RWKV_COLS = SHIFT_W + B_WIDTH
MERGE_COLS = 2 * D_MODEL

ROWS = 128
LANES = 128
N_PAIRS = B_HEADS // 2
NEG = -1e30
VMEM_LIMIT = 56 * 1024 * 1024

F32 = jnp.float32
BF16 = jnp.bfloat16


def _bucket_ranges():
    d = np.arange(0, WINDOW + 1)
    max_exact = N_BUCKETS // 2
    df = np.maximum(d, 1).astype(np.float32)
    large = max_exact + (np.log(df / np.float32(max_exact)) / np.float32(math.log(MAX_DISTANCE / max_exact))
                         * np.float32(N_BUCKETS - max_exact)).astype(np.int32)
    large = np.minimum(large, N_BUCKETS - 1)
    bucket = np.where(d < max_exact, d, large)
    out = []
    for b in range(N_BUCKETS):
        idx = np.nonzero(bucket == b)[0]
        if idx.size:
            assert idx[-1] - idx[0] + 1 == idx.size
            out.append((b, int(idx[0]), int(idx[-1])))
    return tuple(out)


_BUCKET_RANGES = _bucket_ranges()


def _bf(x):
    return x.astype(BF16)


def _mm(a, b):
    return jnp.dot(_bf(a), _bf(b), preferred_element_type=F32)


def _mm_nt(a, b):
    return lax.dot_general(_bf(a), _bf(b), (((1,), (1,)), ((), ())), preferred_element_type=F32)


def _rms(x, g):
    ms = jnp.mean(x * x, axis=-1, keepdims=True)
    return x * lax.rsqrt(ms + NORM_EPS) * g


def _sigmoid(x):
    return 1.0 / (1.0 + jnp.exp(-x))


def _silu(x):
    return x * _sigmoid(x)


def _softplus(x):
    return jnp.maximum(x, 0.0) + jnp.log(1.0 + jnp.exp(-jnp.abs(x)))


def _init_bias(relb_ref, bprev_ref, bcur_ref):
    qi = lax.broadcasted_iota(jnp.int32, (WINDOW, WINDOW), 0)
    kj = lax.broadcasted_iota(jnp.int32, (WINDOW, WINDOW), 1)
    delta = qi - kj

    def body(h, carry):
        bp = jnp.full((WINDOW, WINDOW), NEG, F32)
        bc = jnp.full((WINDOW, WINDOW), NEG, F32)
        for (b, lo, hi) in _BUCKET_RANGES:
            val = relb_ref[b, h]
            bp = jnp.where((delta >= lo - WINDOW) & (delta <= hi - WINDOW), val, bp)
            bc = jnp.where((delta >= lo) & (delta <= hi), val, bc)
        bprev_ref[h] = bp
        bcur_ref[h] = bc
        return carry

    lax.fori_loop(0, A_HEADS, body, 0)


def _attn_core(q, kp, vp, kc, vc, bprev_ref, bcur_ref, sinks_ref, tq, first):
    group = A_HEADS // A_KV_HEADS
    lo_half = lax.broadcasted_iota(jnp.int32, (1, LANES), 1) < HEAD_DIM
    rowi = lax.broadcasted_iota(jnp.int32, (group * tq, 1), 0)
    pieces = [None] * A_HEADS
    for i in range(KV_WIDTH // LANES):
        sl = slice(LANES * i, LANES * (i + 1))
        kpi, vpi, kci, vci = _bf(kp[:, sl]), _bf(vp[:, sl]), _bf(kc[:, sl]), _bf(vc[:, sl])
        for c in range(2):
            kvh = 2 * i + c
            rows = []
            for g in range(group):
                hq = group * kvh + g
                qs = q[:, LANES * (hq // 2):LANES * (hq // 2 + 1)]
                qm = jnp.where(lo_half if hq % 2 == 0 else jnp.logical_not(lo_half), qs, 0.0)
                if hq % 2 != c:
                    qm = pltpu.roll(qm, HEAD_DIM, 1)
                rows.append(qm)
            lhs = _bf(jnp.concatenate(rows, axis=0))
            bp = bprev_ref[group * kvh:group * (kvh + 1), 0:tq, :].reshape(group * tq, WINDOW)
            bc = bcur_ref[group * kvh:group * (kvh + 1), 0:tq, :].reshape(group * tq, WINDOW)
            sp = _mm_nt(lhs, kpi) + bp
            if first is not None:
                sp = jnp.where(first, NEG, sp)
            sc = _mm_nt(lhs, kci) + bc
            sink = sinks_ref[group * kvh + group - 1]
            for g in range(group - 2, -1, -1):
                sink = jnp.where(rowi < (g + 1) * tq, sinks_ref[group * kvh + g], sink)
            m = jnp.maximum(jnp.maximum(jnp.max(sp, axis=-1, keepdims=True),
                                        jnp.max(sc, axis=-1, keepdims=True)), sink)
            pp = jnp.exp(sp - m)
            pc = jnp.exp(sc - m)
            den = jnp.sum(pp, axis=-1, keepdims=True) + jnp.sum(pc, axis=-1, keepdims=True) + jnp.exp(sink - m)
            o = (_mm(pp, vpi) + _mm(pc, vci)) / den
            for g in range(group):
                hq = group * kvh + g
                og = o[g * tq:(g + 1) * tq]
                if hq % 2 != c:
                    og = pltpu.roll(og, HEAD_DIM, 1)
                pieces[hq] = og
    slabs = [jnp.where(lo_half, pieces[2 * s], pieces[2 * s + 1]) for s in range(A_HEADS // 2)]
    return jnp.concatenate(slabs, axis=1)


def _attn_project(x, g_ref, wa_ref):
    h = _rms(x, g_ref[...])
    proj = _mm(h, wa_ref[...])
    q = proj[:, :A_WIDTH] * ATTN_SCALE
    k = proj[:, A_WIDTH:A_WIDTH + KV_WIDTH]
    v = proj[:, A_WIDTH + KV_WIDTH:A_WIDTH + 2 * KV_WIDTH]
    ga = proj[:, A_WIDTH + 2 * KV_WIDTH:]
    return q, k, v, ga


def _attn_prompt_kernel(x_ref, g_ref, wa_ref, woa_ref, relb_ref, sinks_ref,
                        ya_ref, kw_ref, vw_ref, kprev, vprev, bprev, bcur):
    b = pl.program_id(0)
    n = pl.program_id(1)

    @pl.when((b == 0) & (n == 0))
    def _():
        _init_bias(relb_ref, bprev, bcur)

    @pl.when(n == 0)
    def _():
        kprev[...] = jnp.zeros_like(kprev)
        vprev[...] = jnp.zeros_like(vprev)

    q, k, v, ga = _attn_project(x_ref[0], g_ref, wa_ref)
    o = _attn_core(q, kprev[...], vprev[...], k, v, bprev, bcur, sinks_ref, ROWS, n == 0)
    kprev[...] = k
    vprev[...] = v
    kw_ref[0] = k
    vw_ref[0] = v
    ya_ref[0] = _mm(o * _silu(ga), woa_ref[...])


def _attn_sample_kernel(x_ref, g_ref, wa_ref, woa_ref, relb_ref, sinks_ref, kc_ref, vc_ref,
                        ya_ref, ko_ref, vo_ref, qbuf, kbuf, vbuf, obuf, bprev, bcur, *, seq):
    @pl.when(pl.program_id(0) == 0)
    def _():
        _init_bias(relb_ref, bprev, bcur)

    q, k, v, ga = _attn_project(x_ref[...], g_ref, wa_ref)
    qbuf[...] = q
    kbuf[...] = k
    vbuf[...] = v
    pad = jnp.zeros((WINDOW - seq, KV_WIDTH), F32)

    def body(j, carry):
        rows = pl.ds(pl.multiple_of(j * seq, seq), seq)
        kn = kbuf[rows, :]
        vn = vbuf[rows, :]
        kcj = kc_ref[j]
        vcj = vc_ref[j]
        o = _attn_core(qbuf[rows, :], kcj, vcj, jnp.concatenate([kn, pad], axis=0),
                       jnp.concatenate([vn, pad], axis=0), bprev, bcur, sinks_ref, seq, None)
        obuf[rows, :] = o
        ko_ref[j, pl.ds(0, WINDOW - seq), :] = kcj[seq:, :]
        ko_ref[j, pl.ds(WINDOW - seq, seq), :] = kn
        vo_ref[j, pl.ds(0, WINDOW - seq), :] = vcj[seq:, :]
        vo_ref[j, pl.ds(WINDOW - seq, seq), :] = vn
        return carry

    lax.fori_loop(0, ROWS // seq, body, 0)
    ya_ref[...] = _mm(obuf[...] * _silu(ga), woa_ref[...])


def _head_sums(x, ones_blocks):
    outs = []
    width = ones_blocks.shape[0]
    for i in range(x.shape[1] // width):
        xs = x[:, width * i:width * (i + 1)]
        hi = _bf(xs)
        lo = _bf(xs - hi.astype(F32))
        outs.append(jnp.dot(hi, ones_blocks, preferred_element_type=F32)
                    + jnp.dot(lo, ones_blocks, preferred_element_type=F32))
    return jnp.concatenate(outs, axis=1)


def _seg_cumsum(x, seq):
    pos = lax.broadcasted_iota(jnp.int32, (x.shape[0], 1), 0) & (seq - 1)
    s = 1
    while s < seq:
        x = x + jnp.where(pos >= s, pltpu.roll(x, s, 0), 0.0)
        s *= 2
    return x


def _neumann(ps, seq):
    n = ps[0].shape[0]
    ri = lax.broadcasted_iota(jnp.int32, (n, n), 0)
    ci = lax.broadcasted_iota(jnp.int32, (n, n), 1)
    eye = jnp.where(ri == ci, 1.0, 0.0)
    xs = [eye + p for p in ps]
    if seq <= 2:
        return xs
    pws = [_mm(p, p) for p in ps]
    span = 2
    while 2 * span < seq:
        res = [_mm(pw, jnp.concatenate([pw, x], axis=1)) for pw, x in zip(pws, xs)]
        xs = [x + r[:, n:] for x, r in zip(xs, res)]
        pws = [r[:, :n] for r in res]
        span *= 2
    return [x + _mm(pw, x) for pw, x in zip(pws, xs)]


def _rwkv_kernel(*refs, seq, prompt):
    nseq = ROWS // seq
    if prompt:
        (x_ref, g_ref, wr_ref, wob_ref, mu_ref, w0_ref, lora_ref, a0_ref, kk_ref, ka_ref, rk_ref,
         lng_ref, lnb_ref,
         yb_ref, so_ref, sh_ref,
         at_ref, rt_ref, bt_ref, kt_ref, bh_ref, kh_ref, v_ref, gt_ref, r_ref, kp_ref, gb_ref,
         x0_ref, y0_ref, av_ref, t_ref, arbk_ref, s_ref, carry_ref) = refs
        n = pl.program_id(1)
        x = x_ref[0]
    else:
        (x_ref, g_ref, wr_ref, wob_ref, mu_ref, w0_ref, lora_ref, a0_ref, kk_ref, ka_ref, rk_ref,
         lng_ref, lnb_ref, si_ref, shin_ref,
         yb_ref, so_ref, ps_ref,
         at_ref, rt_ref, bt_ref, kt_ref, bh_ref, kh_ref, v_ref, gt_ref, r_ref, kp_ref, gb_ref,
         x0_ref, y0_ref, av_ref, t_ref, arbk_ref, uvt_ref, bkh_ref) = refs
        x = x_ref[...]

    h = _rms(x, g_ref[...])
    proj = _mm(h, wr_ref[...])
    ps = proj[:, :SHIFT_W]
    gb_ref[...] = proj[:, SHIFT_W:]
    pos = lax.broadcasted_iota(jnp.int32, (ROWS, 1), 0) & (seq - 1)
    if prompt:
        @pl.when(n == 0)
        def _():
            carry_ref[...] = jnp.zeros_like(carry_ref)
            s_ref[...] = jnp.zeros_like(s_ref)
        first_rows = carry_ref[...]
    else:
        first_rows = shin_ref[...]
    prev = jnp.where(pos == 0, first_rows, pltpu.roll(ps, 1, 0))
    if prompt:
        carry_ref[...] = ps[ROWS - 1:ROWS, :]
        sh_ref[0] = ps[ROWS - 1:ROWS, :]
    else:
        ps_ref[...] = ps
    z = ps + (prev - ps) * mu_ref[...]
    r = z[:, :B_WIDTH]
    k = z[:, B_WIDTH:2 * B_WIDTH]
    v = z[:, 2 * B_WIDTH:3 * B_WIDTH]
    zl = z[:, 3 * B_WIDTH:]
    lo_half = lax.broadcasted_iota(jnp.int32, (1, LANES), 1) < HEAD_DIM
    lora = _mm(jnp.where(lo_half, jnp.tanh(zl), zl), lora_ref[...])
    w_raw = -_softplus(-(w0_ref[...] + lora[:, :B_WIDTH])) - 0.5
    lw = -jnp.exp(w_raw)
    asig = _sigmoid(a0_ref[...] + lora[:, B_WIDTH:])
    ri = lax.broadcasted_iota(jnp.int32, (2 * LANES, 2 * LANES), 0)
    ci = lax.broadcasted_iota(jnp.int32, (2 * LANES, 2 * LANES), 1)
    ones_blocks = jnp.where((ri >> 6) == (ci >> 6), 1.0, 0.0).astype(BF16)
    kkr = k * kk_ref[...]
    kk = kkr / jnp.maximum(jnp.sqrt(_head_sums(kkr * kkr, ones_blocks)), 1e-12)
    kp = k * (1.0 + (asig - 1.0) * ka_ref[...])
    a_ = -kk
    b_ = kk * asig
    cum = _seg_cumsum(lw, seq)
    if nseq == 1:
        tot = jnp.broadcast_to(cum[ROWS - 1:ROWS, :], (ROWS, B_WIDTH))
    else:
        c3 = cum.reshape(nseq, seq, B_WIDTH)
        tot = jnp.broadcast_to(c3[:, seq - 1:seq, :], (nseq, seq, B_WIDTH)).reshape(ROWS, B_WIDTH)
    g_inv = jnp.exp(-cum)
    g_last = jnp.exp(tot - cum)
    at_ref[...] = a_ * jnp.exp(cum - lw)
    rt_ref[...] = r * jnp.exp(cum)
    bt_ref[...] = b_ * g_inv
    kt_ref[...] = kp * g_inv
    bh_ref[...] = b_ * g_last
    kh_ref[...] = kp * g_last
    gt_ref[...] = jnp.exp(tot)
    v_ref[...] = v
    r_ref[...] = r
    kp_ref[...] = kp

    rr = lax.broadcasted_iota(jnp.int32, (ROWS, ROWS), 0)
    cc = lax.broadcasted_iota(jnp.int32, (ROWS, ROWS), 1)
    shift = int(math.log2(seq))
    same = (rr >> shift) == (cc >> shift)
    strict = same & (rr > cc)
    incl = same & (rr >= cc)
    hi_half = jnp.logical_not(lo_half)
    bd = (rr < HEAD_DIM) == (cc < HEAD_DIM)

    def halves(t):
        return jnp.concatenate([jnp.where(lo_half, t, 0.0), jnp.where(hi_half, t, 0.0)], axis=0)

    a_abs, a_aks = [], []
    for p in range(N_PAIRS):
        sl = slice(LANES * p, LANES * (p + 1))
        ar = jnp.concatenate([at_ref[:, sl], rt_ref[:, sl]], axis=0)
        bk = jnp.concatenate([bt_ref[:, sl], kt_ref[:, sl]], axis=0)
        gm = _mm_nt(halves(ar), bk)
        rbk = []
        for hh in range(2):
            base = 2 * ROWS * hh
            a_abs.append(jnp.where(strict, gm[base:base + ROWS, :ROWS], 0.0))
            a_aks.append(jnp.where(strict, gm[base:base + ROWS, ROWS:], 0.0))
            rbk.append(jnp.where(incl, gm[base + ROWS:base + 2 * ROWS, :ROWS], 0.0))
            rbk.append(jnp.where(incl, gm[base + ROWS:base + 2 * ROWS, ROWS:], 0.0))
        arbk_ref[p] = _bf(jnp.concatenate(rbk, axis=1))
        av_ref[:, sl] = _mm(jnp.concatenate(a_aks[2 * p:2 * p + 2], axis=1), halves(v_ref[:, sl]))
    ts = _neumann(a_abs, seq)
    for p in range(N_PAIRS):
        t_ref[p] = _bf(jnp.concatenate(ts[2 * p:2 * p + 2], axis=1))

    lo64 = lo_half

    def blockdiag(s_nat):
        return jnp.concatenate([jnp.where(lo64, s_nat, 0.0), jnp.where(lo64, 0.0, s_nat)], axis=0)

    pair_slices = [slice(LANES * p, LANES * (p + 1)) for p in range(N_PAIRS)]
    if prompt:
        s0s = [s_ref[p] for p in range(N_PAIRS)]
        x0s = [_mm_nt(at_ref[:, sl], s0) for sl, s0 in zip(pair_slices, s0s)]
        y0s = [_mm_nt(rt_ref[:, sl], s0) for sl, s0 in zip(pair_slices, s0s)]
    else:
        def read_body(j, carry):
            rows = pl.ds(pl.multiple_of(j * seq, seq), seq)
            for p, sl in enumerate(pair_slices):
                s0 = blockdiag(si_ref[j, p])
                arj = jnp.concatenate([at_ref[rows, sl], rt_ref[rows, sl]], axis=0)
                xy = _mm_nt(arj, s0)
                x0_ref[rows, sl] = xy[:seq]
                y0_ref[rows, sl] = xy[seq:]
            return carry
        lax.fori_loop(0, nseq, read_body, 0)
        x0s = [x0_ref[:, sl] for sl in pair_slices]
        y0s = [y0_ref[:, sl] for sl in pair_slices]

    vss = [v_ref[:, sl] for sl in pair_slices]
    us = [jnp.dot(t_ref[p], _bf(halves(x0s[p] + av_ref[:, sl])), preferred_element_type=F32)
          for p, sl in enumerate(pair_slices)]
    ys = []
    for p in range(N_PAIRS):
        u, vs = us[p], vss[p]
        rhs = jnp.concatenate([jnp.where(lo_half, u, 0.0), jnp.where(lo_half, vs, 0.0),
                               jnp.where(hi_half, u, 0.0), jnp.where(hi_half, vs, 0.0)], axis=0)
        ys.append(y0s[p] + jnp.dot(arbk_ref[p], _bf(rhs), preferred_element_type=F32))

    uvts = [jnp.concatenate([us[p], vss[p]], axis=0).T for p in range(N_PAIRS)]
    bkhs = [jnp.concatenate([bh_ref[:, sl], kh_ref[:, sl]], axis=0) for sl in pair_slices]
    if prompt:
        s1s = [s0s[p] * gt_ref[0:1, sl] + jnp.where(bd, _mm(uvts[p], bkhs[p]), 0.0)
               for p, sl in enumerate(pair_slices)]
        for p in range(N_PAIRS):
            s_ref[p] = s1s[p]

        @pl.when(n == pl.num_programs(1) - 1)
        def _():
            for p in range(N_PAIRS):
                so_ref[0, p] = jnp.where(lo64, s1s[p][:HEAD_DIM], s1s[p][HEAD_DIM:])
    else:
        colseq = (lax.broadcasted_iota(jnp.int32, (1, 2 * ROWS), 1) & (ROWS - 1)) >> shift
        for p in range(N_PAIRS):
            uvt_ref[p] = _bf(uvts[p])
            bkh_ref[p] = _bf(bkhs[p])

        def upd_body(j, carry):
            row = pl.ds(pl.multiple_of(j * seq, seq), 1)
            for p, sl in enumerate(pair_slices):
                uvt = jnp.where(colseq == j, uvt_ref[p], jnp.zeros((), BF16))
                upd = jnp.dot(uvt, bkh_ref[p], preferred_element_type=F32)
                so_ref[j, p] = (si_ref[j, p] * gt_ref[row, sl]
                                + jnp.where(lo64, upd[:HEAD_DIM], upd[HEAD_DIM:]))
            return carry
        lax.fori_loop(0, nseq, upd_body, 0)

    y = jnp.concatenate(ys, axis=1)
    mean = _head_sums(y, ones_blocks) * (1.0 / HEAD_DIM)
    dlt = y - mean
    var = _head_sums(dlt * dlt, ones_blocks) * (1.0 / HEAD_DIM)
    yn = dlt * lax.rsqrt(var + GN_EPS) * lng_ref[...] + lnb_ref[...]
    bonus = _head_sums(r_ref[...] * kp_ref[...] * rk_ref[...], ones_blocks) * v_ref[...]
    yo = (yn + bonus) * _silu(gb_ref[...])
    if prompt:
        yb_ref[0] = _mm(yo, wob_ref[...])
    else:
        yb_ref[...] = _mm(yo, wob_ref[...])


def _merge_kernel(x_ref, ya_ref, yb_ref, g_ref, wm_ref, wo_ref, fg_ref, out_ref, *, final):
    x = x_ref[...]
    h = _rms(x, g_ref[...])
    mm = _mm(h, wm_ref[...])
    merged = _sigmoid(mm[:, :D_MODEL]) * ya_ref[...] + _sigmoid(mm[:, D_MODEL:]) * yb_ref[...]
    out = x + _mm(merged, wo_ref[...])
    out_ref[...] = _rms(out, fg_ref[...]) if final else out


def _const_spec(shape):
    nd = len(shape)
    return pl.BlockSpec(shape, lambda *_: (0,) * nd, pipeline_mode=pl.Buffered(1))


def _smem_spec():
    return pl.BlockSpec(memory_space=pltpu.SMEM)


def _params(n_axes):
    return pltpu.CompilerParams(dimension_semantics=("arbitrary",) * n_axes, vmem_limit_bytes=VMEM_LIMIT)


def _attention_prompt(x, g, wa, woa, rel_bias, sinks):
    bsz, t, _ = x.shape
    nb = t // ROWS
    tile = pl.BlockSpec((1, ROWS, D_MODEL), lambda b, n: (b, n, 0))
    win = pl.BlockSpec((1, WINDOW, KV_WIDTH), lambda b, n: (b, 0, 0))
    return pl.pallas_call(
        _attn_prompt_kernel,
        grid=(bsz, nb),
        in_specs=[tile, _const_spec((1, D_MODEL)), _const_spec((D_MODEL, ATTN_COLS)),
                  _const_spec((A_WIDTH, D_MODEL)), _smem_spec(), _smem_spec()],
        out_specs=[tile, win, win],
        out_shape=[jax.ShapeDtypeStruct((bsz, t, D_MODEL), F32),
                   jax.ShapeDtypeStruct((bsz, WINDOW, KV_WIDTH), F32),
                   jax.ShapeDtypeStruct((bsz, WINDOW, KV_WIDTH), F32)],
        scratch_shapes=[pltpu.VMEM((WINDOW, KV_WIDTH), F32), pltpu.VMEM((WINDOW, KV_WIDTH), F32),
                        pltpu.VMEM((A_HEADS, WINDOW, WINDOW), F32), pltpu.VMEM((A_HEADS, WINDOW, WINDOW), F32)],
        compiler_params=_params(2),
        name="attn_prompt",
    )(x, g, wa, woa, rel_bias, sinks)


def _attention_sample(x2d, seq, g, wa, woa, rel_bias, sinks, kc, vc):
    rows = x2d.shape[0]
    nseq = ROWS // seq
    tile = pl.BlockSpec((ROWS, D_MODEL), lambda i: (i, 0))
    cache = pl.BlockSpec((nseq, WINDOW, KV_WIDTH), lambda i: (i, 0, 0))
    return pl.pallas_call(
        functools.partial(_attn_sample_kernel, seq=seq),
        grid=(rows // ROWS,),
        in_specs=[tile, _const_spec((1, D_MODEL)), _const_spec((D_MODEL, ATTN_COLS)),
                  _const_spec((A_WIDTH, D_MODEL)), _smem_spec(), _smem_spec(), cache, cache],
        out_specs=[tile, cache, cache],
        out_shape=[jax.ShapeDtypeStruct((rows, D_MODEL), F32),
                   jax.ShapeDtypeStruct(kc.shape, F32), jax.ShapeDtypeStruct(vc.shape, F32)],
        scratch_shapes=[pltpu.VMEM((ROWS, A_WIDTH), F32), pltpu.VMEM((ROWS, KV_WIDTH), F32),
                        pltpu.VMEM((ROWS, KV_WIDTH), F32), pltpu.VMEM((ROWS, A_WIDTH), F32),
                        pltpu.VMEM((A_HEADS, WINDOW, WINDOW), F32), pltpu.VMEM((A_HEADS, WINDOW, WINDOW), F32)],
        compiler_params=_params(1),
        name="attn_sample",
    )(x2d, g, wa, woa, rel_bias, sinks, kc, vc)


def _rwkv_scratch():
    wide = pltpu.VMEM((ROWS, B_WIDTH), F32)
    return [wide] * 14 + [pltpu.VMEM((N_PAIRS, ROWS, 2 * ROWS), BF16), pltpu.VMEM((N_PAIRS, ROWS, 4 * ROWS), BF16)]


def _rwkv_weight_specs():
    row = _const_spec((1, B_WIDTH))
    return [_const_spec((1, D_MODEL)), _const_spec((D_MODEL, RWKV_COLS)), _const_spec((B_WIDTH, D_MODEL)),
            _const_spec((1, SHIFT_W)), row, _const_spec((LANES, 2 * B_WIDTH)), row, row, row, row, row, row]


def _rwkv_prompt(x, weights):
    bsz, t, _ = x.shape
    nb = t // ROWS
    tile = pl.BlockSpec((1, ROWS, D_MODEL), lambda b, n: (b, n, 0))
    return pl.pallas_call(
        functools.partial(_rwkv_kernel, seq=ROWS, prompt=True),
        grid=(bsz, nb),
        in_specs=[tile] + _rwkv_weight_specs(),
        out_specs=[tile,
                   pl.BlockSpec((1, N_PAIRS, HEAD_DIM, LANES), lambda b, n: (b, 0, 0, 0)),
                   pl.BlockSpec((1, 1, SHIFT_W), lambda b, n: (b, 0, 0))],
        out_shape=[jax.ShapeDtypeStruct((bsz, t, D_MODEL), F32),
                   jax.ShapeDtypeStruct((bsz, N_PAIRS, HEAD_DIM, LANES), F32),
                   jax.ShapeDtypeStruct((bsz, 1, SHIFT_W), F32)],
        scratch_shapes=_rwkv_scratch() + [pltpu.VMEM((N_PAIRS, LANES, LANES), F32),
                                          pltpu.VMEM((1, SHIFT_W), F32)],
        compiler_params=_params(2),
        name="rwkv_prompt",
    )(x, *weights)


def _rwkv_sample(x2d, seq, weights, s_nat, shift_rows):
    rows = x2d.shape[0]
    nseq = ROWS // seq
    tile = pl.BlockSpec((ROWS, D_MODEL), lambda i: (i, 0))
    state = pl.BlockSpec((nseq, N_PAIRS, HEAD_DIM, LANES), lambda i: (i, 0, 0, 0))
    shift = pl.BlockSpec((ROWS, SHIFT_W), lambda i: (i, 0))
    return pl.pallas_call(
        functools.partial(_rwkv_kernel, seq=seq, prompt=False),
        grid=(rows // ROWS,),
        in_specs=[tile] + _rwkv_weight_specs() + [state, shift],
        out_specs=[tile, state, shift],
        out_shape=[jax.ShapeDtypeStruct((rows, D_MODEL), F32),
                   jax.ShapeDtypeStruct(s_nat.shape, F32),
                   jax.ShapeDtypeStruct((rows, SHIFT_W), F32)],
        scratch_shapes=_rwkv_scratch() + [pltpu.VMEM((N_PAIRS, ROWS, 2 * ROWS), BF16),
                                          pltpu.VMEM((N_PAIRS, 2 * ROWS, LANES), BF16)],
        compiler_params=_params(1),
        name="rwkv_sample",
    )(x2d, *weights, s_nat, shift_rows)


def _merge(x2d, ya, yb, g, wm, wo, fg, final, name):
    rows = x2d.shape[0]
    tm = 256
    tile = pl.BlockSpec((tm, D_MODEL), lambda i: (i, 0))
    return pl.pallas_call(
        functools.partial(_merge_kernel, final=final),
        grid=(rows // tm,),
        in_specs=[tile, tile, tile, _const_spec((1, D_MODEL)), _const_spec((D_MODEL, MERGE_COLS)),
                  _const_spec((D_MODEL, D_MODEL)), _const_spec((1, D_MODEL))],
        out_specs=tile,
        out_shape=jax.ShapeDtypeStruct((rows, D_MODEL), F32),
        compiler_params=_params(1),
        name=name,
    )(x2d, ya, yb, g, wm, wo, fg)


def _pairs_from_heads(s):
    b = s.shape[0]
    return s.reshape(b, N_PAIRS, 2, HEAD_DIM, HEAD_DIM).transpose(0, 1, 3, 2, 4).reshape(b, N_PAIRS, HEAD_DIM, LANES)


def _heads_from_pairs(s):
    b = s.shape[0]
    return s.reshape(b, N_PAIRS, HEAD_DIM, 2, HEAD_DIM).transpose(0, 1, 3, 2, 4).reshape(b, B_HEADS, HEAD_DIM, HEAD_DIM)


def kernel(x_prompt, x_sample, cache_k_win, cache_v_win, state_wkv, state_shift, rel_bias, norm_g, w_in, attn_sinks, shift_mu, rwkv_w0, rwkv_w2, rwkv_a0, rwkv_a2, rwkv_k_k, rwkv_k_a, rwkv_r_k, lnx_g, lnx_b, w_out_a, w_out_b, w_o, final_g):
    depth = w_in.shape[0]
    bsz, t, _ = x_prompt.shape
    dbsz, dseq, _ = x_sample.shape
    hp = x_prompt
    hs = x_sample.reshape(dbsz * dseq, D_MODEL)
    fg = final_g.reshape(1, D_MODEL)
    outs = [[] for _ in range(8)]
    for l in range(depth):
        g = norm_g[l].reshape(1, D_MODEL)
        w = w_in[l]
        wa = _bf(w[:, :ATTN_COLS])
        wr = _bf(w[:, ATTN_COLS:ATTN_COLS + RWKV_COLS])
        wm = _bf(w[:, ATTN_COLS + RWKV_COLS:])
        woa, wob, wo = _bf(w_out_a[l]), _bf(w_out_b[l]), _bf(w_o[l])
        zeros = jnp.zeros((DECAY_LORA, B_WIDTH), F32)
        lora = _bf(jnp.concatenate([jnp.concatenate([rwkv_w2[l], zeros], axis=1),
                                    jnp.concatenate([zeros, rwkv_a2[l]], axis=1)], axis=0))
        rw = [g, wr, wob, shift_mu[l].reshape(1, SHIFT_W), rwkv_w0[l].reshape(1, B_WIDTH), lora,
              rwkv_a0[l].reshape(1, B_WIDTH), rwkv_k_k[l].reshape(1, B_WIDTH), rwkv_k_a[l].reshape(1, B_WIDTH),
              rwkv_r_k[l].reshape(1, B_WIDTH), lnx_g[l].reshape(1, B_WIDTH), lnx_b[l].reshape(1, B_WIDTH)]
        sinks = attn_sinks[l]

        ya_p, k1, v1 = _attention_prompt(hp, g, wa, woa, rel_bias, sinks)
        yb_p, s1, t1 = _rwkv_prompt(hp, rw)
        hp = _merge(hp.reshape(bsz * t, D_MODEL), ya_p.reshape(bsz * t, D_MODEL), yb_p.reshape(bsz * t, D_MODEL),
                    g, wm, wo, fg, l == depth - 1, "merge_prompt").reshape(bsz, t, D_MODEL)

        kc = cache_k_win[l].reshape(dbsz, WINDOW, KV_WIDTH)
        vc = cache_v_win[l].reshape(dbsz, WINDOW, KV_WIDTH)
        ya_s, k2, v2 = _attention_sample(hs, dseq, g, wa, woa, rel_bias, sinks, kc, vc)
        shift_rows = jnp.repeat(state_shift[l], dseq, axis=0)
        yb_s, s2, ps_s = _rwkv_sample(hs, dseq, rw, _pairs_from_heads(state_wkv[l]), shift_rows)
        hs = _merge(hs, ya_s, yb_s, g, wm, wo, fg, l == depth - 1, "merge_sample")

        outs[0].append(k1.reshape(bsz, WINDOW, A_KV_HEADS, HEAD_DIM))
        outs[1].append(v1.reshape(bsz, WINDOW, A_KV_HEADS, HEAD_DIM))
        outs[2].append(_heads_from_pairs(s1))
        outs[3].append(t1.reshape(bsz, SHIFT_W))
        outs[4].append(k2.reshape(dbsz, WINDOW, A_KV_HEADS, HEAD_DIM))
        outs[5].append(v2.reshape(dbsz, WINDOW, A_KV_HEADS, HEAD_DIM))
        outs[6].append(_heads_from_pairs(s2))
        outs[7].append(ps_s.reshape(dbsz, dseq, SHIFT_W)[:, -1])
    y_prompt = hp
    y_sample = hs.reshape(dbsz, dseq, D_MODEL)
    return (y_prompt, y_sample) + tuple(jnp.stack(o) for o in outs)
```

```python
import functools
import math

import numpy as np
import jax
import jax.numpy as jnp
from jax import lax
from jax.experimental import pallas as pl
from jax.experimental.pallas import tpu as pltpu

D_MODEL = 1024
HEAD_DIM = 64
A_HEADS = 16
A_KV_HEADS = 4
A_WIDTH = A_HEADS * HEAD_DIM
KV_WIDTH = A_KV_HEADS * HEAD_DIM
WINDOW = 128
ATTN_SCALE = HEAD_DIM ** -0.5
N_BUCKETS = 32
MAX_DISTANCE = 128
B_HEADS = 16
B_WIDTH = B_HEADS * HEAD_DIM
DECAY_LORA = 64
A_LORA = 64
SHIFT_W = 3 * B_WIDTH + DECAY_LORA + A_LORA
GN_EPS = 64e-5
NORM_EPS = 1e-6

ATTN_COLS = 2 * A_WIDTH + 2 * KV_WIDTH
RWKV_COLS = SHIFT_W + B_WIDTH
MERGE_COLS = 2 * D_MODEL

ROWS = 128
LANES = 128
N_PAIRS = B_HEADS // 2
NEG = -1e30
VMEM_LIMIT = 56 * 1024 * 1024

F32 = jnp.float32
BF16 = jnp.bfloat16


def _bucket_ranges():
    d = np.arange(0, WINDOW + 1)
    max_exact = N_BUCKETS // 2
    df = np.maximum(d, 1).astype(np.float32)
    large = max_exact + (np.log(df / np.float32(max_exact)) / np.float32(math.log(MAX_DISTANCE / max_exact))
                         * np.float32(N_BUCKETS - max_exact)).astype(np.int32)
    large = np.minimum(large, N_BUCKETS - 1)
    bucket = np.where(d < max_exact, d, large)
    out = []
    for b in range(N_BUCKETS):
        idx = np.nonzero(bucket == b)[0]
        if idx.size:
            assert idx[-1] - idx[0] + 1 == idx.size
            out.append((b, int(idx[0]), int(idx[-1])))
    return tuple(out)


_BUCKET_RANGES = _bucket_ranges()


def _bf(x):
    return x.astype(BF16)


def _mm(a, b):
    return jnp.dot(_bf(a), _bf(b), preferred_element_type=F32)


def _mm_nt(a, b):
    return lax.dot_general(_bf(a), _bf(b), (((1,), (1,)), ((), ())), preferred_element_type=F32)


def _rms(x, g):
    ms = jnp.mean(x * x, axis=-1, keepdims=True)
    return x * lax.rsqrt(ms + NORM_EPS) * g


def _sigmoid(x):
    return 1.0 / (1.0 + jnp.exp(-x))


def _silu(x):
    return x * _sigmoid(x)


def _init_bias(relb_ref, bprev_ref, bcur_ref):
    qi = lax.broadcasted_iota(jnp.int32, (WINDOW, WINDOW), 0)
    kj = lax.broadcasted_iota(jnp.int32, (WINDOW, WINDOW), 1)
    delta = qi - kj

    def body(h, carry):
        bp = jnp.full((WINDOW, WINDOW), NEG, F32)
        bc = jnp.full((WINDOW, WINDOW), NEG, F32)
        for (b, lo, hi) in _BUCKET_RANGES:
            val = relb_ref[b, h]
            bp = jnp.where((delta >= lo - WINDOW) & (delta <= hi - WINDOW), val, bp)
            bc = jnp.where((delta >= lo) & (delta <= hi), val, bc)
        bprev_ref[h] = bp
        bcur_ref[h] = bc
        return carry

    lax.fori_loop(0, A_HEADS, body, 0)


def _attn_core(q, kp, vp, kc, vc, bprev_ref, bcur_ref, sinks_ref, tq, first):
    group = A_HEADS // A_KV_HEADS
    lo_half = lax.broadcasted_iota(jnp.int32, (1, LANES), 1) < HEAD_DIM
    hi_half = jnp.logical_not(lo_half)
    rowi = lax.broadcasted_iota(jnp.int32, (group * tq, 1), 0)
    sps, scs, sinks = [], [], []
    for kvh in range(A_KV_HEADS):
        i, c = divmod(kvh, 2)
        sl = slice(LANES * i, LANES * (i + 1))
        rows = []
        for g in range(group):
            hq = group * kvh + g
            qs = q[:, LANES * (hq // 2):LANES * (hq // 2 + 1)]
            qm = jnp.where(lo_half if hq % 2 == 0 else hi_half, qs, 0.0)
            if hq % 2 != c:
                qm = pltpu.roll(qm, HEAD_DIM, 1)
            rows.append(qm)
        lhs = _bf(jnp.concatenate(rows, axis=0))
        bp = bprev_ref[group * kvh:group * (kvh + 1), 0:tq, :].reshape(group * tq, WINDOW)
        bc = bcur_ref[group * kvh:group * (kvh + 1), 0:tq, :].reshape(group * tq, WINDOW)
        sp = _mm_nt(lhs, kp[:, sl]) + bp
        if first is not None:
            sp = jnp.where(first, NEG, sp)
        sps.append(sp)
        scs.append(_mm_nt(lhs, kc[:, sl]) + bc)
        sink = sinks_ref[group * kvh + group - 1]
        for g in range(group - 2, -1, -1):
            sink = jnp.where(rowi < (g + 1) * tq, sinks_ref[group * kvh + g], sink)
        sinks.append(sink)
    ms = [jnp.maximum(jnp.max(jnp.maximum(sp, sc), axis=-1, keepdims=True), sink)
          for sp, sc, sink in zip(sps, scs, sinks)]
    pps = [jnp.exp(sp - m) for sp, m in zip(sps, ms)]
    pcs = [jnp.exp(sc - m) for sc, m in zip(scs, ms)]
    outs = []
    for kvh in range(A_KV_HEADS):
        i, c = divmod(kvh, 2)
        sl = slice(LANES * i, LANES * (i + 1))
        own = lo_half if c == 0 else hi_half
        o = _mm(pps[kvh], jnp.where(own, vp[:, sl], 1.0)) + _mm(pcs[kvh], jnp.where(own, vc[:, sl], 1.0))
        den = pltpu.roll(o, HEAD_DIM, 1) + jnp.exp(sinks[kvh] - ms[kvh])
        outs.append(o / den)
    pieces = [None] * A_HEADS
    for kvh in range(A_KV_HEADS):
        for g in range(group):
            hq = group * kvh + g
            og = outs[kvh][g * tq:(g + 1) * tq]
            if hq % 2 != kvh % 2:
                og = pltpu.roll(og, HEAD_DIM, 1)
            pieces[hq] = og
    slabs = [jnp.where(lo_half, pieces[2 * s], pieces[2 * s + 1]) for s in range(A_HEADS // 2)]
    return jnp.concatenate(slabs, axis=1)


def _attn_project(x, g_ref, wa_ref):
    h = _rms(x, g_ref[...])
    proj = _mm(h, wa_ref[...])
    q = proj[:, :A_WIDTH] * ATTN_SCALE
    k = proj[:, A_WIDTH:A_WIDTH + KV_WIDTH]
    v = proj[:, A_WIDTH + KV_WIDTH:A_WIDTH + 2 * KV_WIDTH]
    ga = proj[:, A_WIDTH + 2 * KV_WIDTH:]
    return q, k, v, ga


def _attn_prompt_kernel(x_ref, g_ref, wa_ref, woa_ref, relb_ref, sinks_ref,
                        ya_ref, kw_ref, vw_ref, kprev, vprev, bprev, bcur):
    b = pl.program_id(0)
    n = pl.program_id(1)

    @pl.when((b == 0) & (n == 0))
    def _():
        _init_bias(relb_ref, bprev, bcur)

    @pl.when(n == 0)
    def _():
        kprev[...] = jnp.zeros_like(kprev)
        vprev[...] = jnp.zeros_like(vprev)

    q, k, v, ga = _attn_project(x_ref[0], g_ref, wa_ref)
    o = _attn_core(q, kprev[...], vprev[...], k, v, bprev, bcur, sinks_ref, ROWS, n == 0)
    kprev[...] = k
    vprev[...] = v
    kw_ref[0] = k
    vw_ref[0] = v
    ya_ref[0] = _mm(o * _silu(ga), woa_ref[...])


def _attn_sample_kernel(x_ref, g_ref, wa_ref, woa_ref, relb_ref, sinks_ref, kc_ref, vc_ref,
                        ya_ref, ko_ref, vo_ref, qbuf, kbuf, vbuf, obuf, bprev, bcur, *, seq):
    @pl.when(pl.program_id(0) == 0)
    def _():
        _init_bias(relb_ref, bprev, bcur)

    q, k, v, ga = _attn_project(x_ref[...], g_ref, wa_ref)
    qbuf[...] = q
    kbuf[...] = k
    vbuf[...] = v
    pad = jnp.zeros((WINDOW - seq, KV_WIDTH), F32)

    def body(j, carry):
        rows = pl.ds(pl.multiple_of(j * seq, seq), seq)
        kn = kbuf[rows, :]
        vn = vbuf[rows, :]
        kcj = kc_ref[j]
        vcj = vc_ref[j]
        o = _attn_core(qbuf[rows, :], kcj, vcj, jnp.concatenate([kn, pad], axis=0),
                       jnp.concatenate([vn, pad], axis=0), bprev, bcur, sinks_ref, seq, None)
        obuf[rows, :] = o
        ko_ref[j, pl.ds(0, WINDOW - seq), :] = kcj[seq:, :]
        ko_ref[j, pl.ds(WINDOW - seq, seq), :] = kn
        vo_ref[j, pl.ds(0, WINDOW - seq), :] = vcj[seq:, :]
        vo_ref[j, pl.ds(WINDOW - seq, seq), :] = vn
        return carry

    lax.fori_loop(0, ROWS // seq, body, 0)
    ya_ref[...] = _mm(obuf[...] * _silu(ga), woa_ref[...])


def _head_sums(x, ones_blocks):
    outs = []
    width = ones_blocks.shape[0]
    for i in range(x.shape[1] // width):
        xs = x[:, width * i:width * (i + 1)]
        hi = _bf(xs)
        lo = _bf(xs - hi.astype(F32))
        outs.append(jnp.dot(hi, ones_blocks, preferred_element_type=F32)
                    + jnp.dot(lo, ones_blocks, preferred_element_type=F32))
    return jnp.concatenate(outs, axis=1)


def _seg_cumsum(x, seq):
    pos = lax.broadcasted_iota(jnp.int32, (x.shape[0], 1), 0) & (seq - 1)
    s = 1
    while s < seq:
        x = x + jnp.where(pos >= s, pltpu.roll(x, s, 0), 0.0)
        s *= 2
    return x


def _neumann_levels(ps, seq, out):
    n = ps[0].shape[0]
    ri = lax.broadcasted_iota(jnp.int32, (n, n), 0)
    ci = lax.broadcasted_iota(jnp.int32, (n, n), 1)
    eye = jnp.where(ri == ci, 1.0, 0.0)
    xs = [eye + p for p in ps]
    if seq > 2:
        pws = [_mm(p, p) for p in ps]
        yield
        span = 2
        while span < seq:
            last = 2 * span >= seq
            skip = span if span >= 16 else 0
            nxt_x, nxt_p = [], []
            for pw, x in zip(pws, xs):
                rhs = x if last else jnp.concatenate([pw, x], axis=1)
                res = _mm(pw[skip:], rhs)
                dx = res if last else res[:, n:]
                if skip:
                    nxt_x.append(jnp.concatenate([x[:skip], x[skip:] + dx], axis=0))
                else:
                    nxt_x.append(x + dx)
                if not last:
                    pn = res[:, :n]
                    nxt_p.append(jnp.concatenate([jnp.zeros((skip, n), F32), pn], axis=0) if skip else pn)
            xs, pws = nxt_x, nxt_p
            span *= 2
            yield
    out[:] = xs


PREP_NAMES = ("at", "rt", "bt", "kt", "bh", "kh", "v", "gt", "bonus", "sgb")
RWKV_WEIGHT_NAMES = ("g", "wr", "wob", "mu", "w0", "lora", "a0", "kk", "ka", "rk", "lng", "lnb")
EARLY_NAMES = ("at", "rt", "bt", "kt")
LATE_NAMES = tuple(n for n in PREP_NAMES if n not in EARLY_NAMES)
PREP_SLAB = 2 * LANES


def _ones_blocks():
    ri = lax.broadcasted_iota(jnp.int32, (2 * LANES, 2 * LANES), 0)
    ci = lax.broadcasted_iota(jnp.int32, (2 * LANES, 2 * LANES), 1)
    return jnp.where((ri >> 6) == (ci >> 6), 1.0, 0.0).astype(BF16)


def _rwkv_prep(x, first_ref, w, seq, out, ps_ref, lora_ref):
    nseq = ROWS // seq
    h = _rms(x, w["g"][...])
    proj = _mm(h, w["wr"][...])
    ps_ref[...] = proj[:, :SHIFT_W]
    out["sgb"][...] = _silu(proj[:, SHIFT_W:])
    pos = lax.broadcasted_iota(jnp.int32, (ROWS, 1), 0) & (seq - 1)

    def shifted(cols):
        cur = ps_ref[:, cols]
        prev = jnp.where(pos == 0, first_ref[:, cols], pltpu.roll(cur, 1, 0))
        return cur + (prev - cur) * w["mu"][:, cols]

    zl = shifted(slice(3 * B_WIDTH, SHIFT_W))
    lo_half = lax.broadcasted_iota(jnp.int32, (1, LANES), 1) < HEAD_DIM
    lora_ref[...] = _mm(jnp.where(lo_half, jnp.tanh(zl), zl), w["lora"][...])
    yield
    ones_blocks = _ones_blocks()
    for d in range(B_WIDTH // PREP_SLAB):
        cols = slice(PREP_SLAB * d, PREP_SLAB * (d + 1))

        def wcols(name):
            return w[name][:, cols]

        r = shifted(cols)
        k = shifted(slice(B_WIDTH + cols.start, B_WIDTH + cols.stop))
        v = shifted(slice(2 * B_WIDTH + cols.start, 2 * B_WIDTH + cols.stop))
        lw = -math.exp(-0.5) * _sigmoid(wcols("w0") + lora_ref[:, cols])
        asig = _sigmoid(wcols("a0") + lora_ref[:, slice(B_WIDTH + cols.start, B_WIDTH + cols.stop)])
        kkr = k * wcols("kk")
        kk = kkr / jnp.maximum(jnp.sqrt(_head_sums(kkr * kkr, ones_blocks)), 1e-12)
        kp = k * (1.0 + (asig - 1.0) * wcols("ka"))
        a_ = -kk
        b_ = kk * asig
        cum = _seg_cumsum(lw, seq)
        if nseq == 1:
            tot = jnp.broadcast_to(cum[ROWS - 1:ROWS, :], cum.shape)
        else:
            c3 = cum.reshape(nseq, seq, PREP_SLAB)
            tot = jnp.broadcast_to(c3[:, seq - 1:seq, :], c3.shape).reshape(cum.shape)
        g_inv = jnp.exp(-cum)
        g_last = jnp.exp(tot - cum)
        out["at"][:, cols] = a_ * jnp.exp(cum - lw)
        out["rt"][:, cols] = r * jnp.exp(cum)
        out["bt"][:, cols] = b_ * g_inv
        out["kt"][:, cols] = kp * g_inv
        out["bh"][:, cols] = b_ * g_last
        out["kh"][:, cols] = kp * g_last
        out["gt"][:, cols] = jnp.exp(tot)
        out["v"][:, cols] = v
        out["bonus"][:, cols] = _head_sums(r * kp * wcols("rk"), ones_blocks) * v
        yield


def _rwkv_core(buf, w, seq, av_ref, t_ref, arbk_ref, res, *, s_ref=None, si_ref=None, so_ref=None,
               x0_ref=None, y0_ref=None, uvt_ref=None, bkh_ref=None):
    nseq = ROWS // seq
    prompt = s_ref is not None
    lo_half = lax.broadcasted_iota(jnp.int32, (1, LANES), 1) < HEAD_DIM
    hi_half = jnp.logical_not(lo_half)
    rr = lax.broadcasted_iota(jnp.int32, (ROWS, ROWS), 0)
    cc = lax.broadcasted_iota(jnp.int32, (ROWS, ROWS), 1)
    shift = int(math.log2(seq))
    same = (rr >> shift) == (cc >> shift)
    strict = same & (rr > cc)
    incl = same & (rr >= cc)
    bd = (rr < HEAD_DIM) == (cc < HEAD_DIM)
    pair_slices = [slice(LANES * p, LANES * (p + 1)) for p in range(N_PAIRS)]
    at_ref, rt_ref, bt_ref, kt_ref, bh_ref, kh_ref, v_ref, gt_ref = (
        buf[n] for n in ("at", "rt", "bt", "kt", "bh", "kh", "v", "gt"))

    def halves(t):
        return jnp.concatenate([jnp.where(lo_half, t, 0.0), jnp.where(hi_half, t, 0.0)], axis=0)

    def blockdiag(s_nat):
        return jnp.concatenate([jnp.where(lo_half, s_nat, 0.0), jnp.where(lo_half, 0.0, s_nat)], axis=0)

    if prompt:
        s0s = [s_ref[p] for p in range(N_PAIRS)]
        x0s = [_mm_nt(at_ref[:, sl], s0) for sl, s0 in zip(pair_slices, s0s)]
        y0s = [_mm_nt(rt_ref[:, sl], s0) for sl, s0 in zip(pair_slices, s0s)]
    else:
        def read_body(j, carry):
            rows = pl.ds(pl.multiple_of(j * seq, seq), seq)
            for p, sl in enumerate(pair_slices):
                s0 = blockdiag(si_ref[j, p])
                arj = jnp.concatenate([at_ref[rows, sl], rt_ref[rows, sl]], axis=0)
                xy = _mm_nt(arj, s0)
                x0_ref[rows, sl] = xy[:seq]
                y0_ref[rows, sl] = xy[seq:]
            return carry
        lax.fori_loop(0, nseq, read_body, 0)
        x0s = [x0_ref[:, sl] for sl in pair_slices]
        y0s = [y0_ref[:, sl] for sl in pair_slices]
    yield

    a_abs, a_aks = [], []
    for p, sl in enumerate(pair_slices):
        ar = jnp.concatenate([at_ref[:, sl], rt_ref[:, sl]], axis=0)
        bk = jnp.concatenate([bt_ref[:, sl], kt_ref[:, sl]], axis=0)
        gm = _mm_nt(halves(ar), bk)
        rbk = []
        for hh in range(2):
            base = 2 * ROWS * hh
            a_abs.append(jnp.where(strict, gm[base:base + ROWS, :ROWS], 0.0))
            a_aks.append(jnp.where(strict, gm[base:base + ROWS, ROWS:], 0.0))
            rbk.append(jnp.where(incl, gm[base + ROWS:base + 2 * ROWS, :ROWS], 0.0))
            rbk.append(jnp.where(incl, gm[base + ROWS:base + 2 * ROWS, ROWS:], 0.0))
        arbk_ref[p] = _bf(jnp.concatenate(rbk, axis=1))
        av_ref[:, sl] = _mm(jnp.concatenate(a_aks[2 * p:2 * p + 2], axis=1), halves(v_ref[:, sl]))
        if p % 4 == 3:
            yield
    ts = []
    yield from _neumann_levels(a_abs, seq, ts)
    for p in range(N_PAIRS):
        t_ref[p] = _bf(jnp.concatenate(ts[2 * p:2 * p + 2], axis=1))

    vss = [v_ref[:, sl] for sl in pair_slices]
    us = [jnp.dot(t_ref[p], _bf(halves(x0s[p] + av_ref[:, sl])), preferred_element_type=F32)
          for p, sl in enumerate(pair_slices)]
    yield
    ys = []
    for p in range(N_PAIRS):
        u, vs = us[p], vss[p]
        rhs = jnp.concatenate([jnp.where(lo_half, u, 0.0), jnp.where(lo_half, vs, 0.0),
                               jnp.where(hi_half, u, 0.0), jnp.where(hi_half, vs, 0.0)], axis=0)
        ys.append(y0s[p] + jnp.dot(arbk_ref[p], _bf(rhs), preferred_element_type=F32))
    yield

    uvts = [jnp.concatenate([us[p], vss[p]], axis=0).T for p in range(N_PAIRS)]
    bkhs = [jnp.concatenate([bh_ref[:, sl], kh_ref[:, sl]], axis=0) for sl in pair_slices]
    if prompt:
        s1s = [s0s[p] * gt_ref[0:1, sl] + jnp.where(bd, _mm(uvts[p], bkhs[p]), 0.0)
               for p, sl in enumerate(pair_slices)]
        for p in range(N_PAIRS):
            s_ref[p] = s1s[p]
        res["s1"] = s1s
    else:
        colseq = (lax.broadcasted_iota(jnp.int32, (1, 2 * ROWS), 1) & (ROWS - 1)) >> shift
        for p in range(N_PAIRS):
            uvt_ref[p] = _bf(uvts[p])
            bkh_ref[p] = _bf(bkhs[p])

        def upd_body(j, carry):
            row = pl.ds(pl.multiple_of(j * seq, seq), 1)
            for p, sl in enumerate(pair_slices):
                uvt = jnp.where(colseq == j, uvt_ref[p], jnp.zeros((), BF16))
                upd = jnp.dot(uvt, bkh_ref[p], preferred_element_type=F32)
                so_ref[j, p] = (si_ref[j, p] * gt_ref[row, sl]
                                + jnp.where(lo_half, upd[:HEAD_DIM], upd[HEAD_DIM:]))
            return carry
        lax.fori_loop(0, nseq, upd_body, 0)
    yield

    ones_blocks = _ones_blocks()
    y = jnp.concatenate(ys, axis=1)
    mean = _head_sums(y, ones_blocks) * (1.0 / HEAD_DIM)
    dlt = y - mean
    var = _head_sums(dlt * dlt, ones_blocks) * (1.0 / HEAD_DIM)
    yn = dlt * lax.rsqrt(var + GN_EPS) * w["lng"][...] + w["lnb"][...]
    yo = (yn + buf["bonus"][...]) * buf["sgb"][...]
    res["yb"] = _mm(yo, w["wob"][...])


N_RWKV_W = len(RWKV_WEIGHT_NAMES)
N_PREP = len(PREP_NAMES)
PROMPT_ORDER = "CCCCCPCCPCCPCPCPCC"


def _rwkv_prompt_kernel(*refs):
    x_ref = refs[0]
    w = dict(zip(RWKV_WEIGHT_NAMES, refs[1:1 + N_RWKV_W]))
    yb_ref, so_ref, sh_ref = refs[1 + N_RWKV_W:4 + N_RWKV_W]
    scratch = refs[4 + N_RWKV_W:]
    cur = dict(zip(PREP_NAMES, scratch[:N_PREP]))
    nxt = dict(cur)
    nxt.update(zip(LATE_NAMES, scratch[N_PREP:N_PREP + len(LATE_NAMES)]))
    av_ref, lora_ref, t_ref, arbk_ref, s_ref, ps_ref, carry_ref = scratch[N_PREP + len(LATE_NAMES):]
    b = pl.program_id(0)
    i = pl.program_id(1)

    @pl.when((b == 0) & (i == 0))
    def _():
        for name in PREP_NAMES:
            cur[name][...] = jnp.zeros_like(cur[name])

    @pl.when(i == 0)
    def _():
        carry_ref[...] = jnp.zeros_like(carry_ref)

    @pl.when(i <= 1)
    def _():
        s_ref[...] = jnp.zeros_like(s_ref)

    res = {}
    prep = _rwkv_prep(x_ref[0], carry_ref, w, ROWS, nxt, ps_ref, lora_ref)
    core = _rwkv_core(cur, w, ROWS, av_ref, t_ref, arbk_ref, res, s_ref=s_ref)
    for kind in PROMPT_ORDER:
        next(prep if kind == "P" else core, None)
    for _ in prep:
        pass
    for _ in core:
        pass
    carry_ref[...] = ps_ref[ROWS - 1:ROWS, :]
    sh_ref[0] = ps_ref[ROWS - 1:ROWS, :]
    yb_ref[0] = res["yb"]

    @pl.when(i == pl.num_programs(1) - 1)
    def _():
        lo_half = lax.broadcasted_iota(jnp.int32, (1, LANES), 1) < HEAD_DIM
        for p in range(N_PAIRS):
            so_ref[0, p] = jnp.where(lo_half, res["s1"][p][:HEAD_DIM], res["s1"][p][HEAD_DIM:])

    for name in LATE_NAMES:
        cur[name][...] = nxt[name][...]


def _rwkv_sample_kernel(*refs, seq):
    x_ref = refs[0]
    w = dict(zip(RWKV_WEIGHT_NAMES, refs[1:1 + N_RWKV_W]))
    si_ref, shin_ref, yb_ref, so_ref, ps_ref = refs[1 + N_RWKV_W:6 + N_RWKV_W]
    scratch = refs[6 + N_RWKV_W:]
    buf = dict(zip(PREP_NAMES, scratch[:N_PREP]))
    av_ref, lora_ref, t_ref, arbk_ref, x0_ref, y0_ref, uvt_ref, bkh_ref = scratch[N_PREP:]
    res = {}
    for _ in _rwkv_prep(x_ref[...], shin_ref, w, seq, buf, ps_ref, lora_ref):
        pass
    for _ in _rwkv_core(buf, w, seq, av_ref, t_ref, arbk_ref, res, si_ref=si_ref, so_ref=so_ref,
                        x0_ref=x0_ref, y0_ref=y0_ref, uvt_ref=uvt_ref, bkh_ref=bkh_ref):
        pass
    yb_ref[...] = res["yb"]


def _merge_kernel(x_ref, ya_ref, yb_ref, g_ref, wm_ref, wo_ref, fg_ref, out_ref, *, final):
    x = x_ref[...]
    h = _rms(x, g_ref[...])
    mm = _mm(h, wm_ref[...])
    merged = _sigmoid(mm[:, :D_MODEL]) * ya_ref[...] + _sigmoid(mm[:, D_MODEL:]) * yb_ref[...]
    out = x + _mm(merged, wo_ref[...])
    out_ref[...] = _rms(out, fg_ref[...]) if final else out


def _const_spec(shape):
    nd = len(shape)
    return pl.BlockSpec(shape, lambda *_: (0,) * nd, pipeline_mode=pl.Buffered(1))


def _smem_spec():
    return pl.BlockSpec(memory_space=pltpu.SMEM)


def _params(n_axes):
    return pltpu.CompilerParams(dimension_semantics=("arbitrary",) * n_axes, vmem_limit_bytes=VMEM_LIMIT)


def _attention_prompt(x, g, wa, woa, rel_bias, sinks):
    bsz, t, _ = x.shape
    nb = t // ROWS
    tile = pl.BlockSpec((1, ROWS, D_MODEL), lambda b, n: (b, n, 0))
    win = pl.BlockSpec((1, WINDOW, KV_WIDTH), lambda b, n: (b, 0, 0))
    return pl.pallas_call(
        _attn_prompt_kernel,
        grid=(bsz, nb),
        in_specs=[tile, _const_spec((1, D_MODEL)), _const_spec((D_MODEL, ATTN_COLS)),
                  _const_spec((A_WIDTH, D_MODEL)), _smem_spec(), _smem_spec()],
        out_specs=[tile, win, win],
        out_shape=[jax.ShapeDtypeStruct((bsz, t, D_MODEL), F32),
                   jax.ShapeDtypeStruct((bsz, WINDOW, KV_WIDTH), F32),
                   jax.ShapeDtypeStruct((bsz, WINDOW, KV_WIDTH), F32)],
        scratch_shapes=[pltpu.VMEM((WINDOW, KV_WIDTH), F32), pltpu.VMEM((WINDOW, KV_WIDTH), F32),
                        pltpu.VMEM((A_HEADS, WINDOW, WINDOW), F32), pltpu.VMEM((A_HEADS, WINDOW, WINDOW), F32)],
        compiler_params=_params(2),
        name="attn_prompt",
    )(x, g, wa, woa, rel_bias, sinks)


def _attention_sample(x2d, seq, g, wa, woa, rel_bias, sinks, kc, vc):
    rows = x2d.shape[0]
    nseq = ROWS // seq
    tile = pl.BlockSpec((ROWS, D_MODEL), lambda i: (i, 0))
    cache = pl.BlockSpec((nseq, WINDOW, KV_WIDTH), lambda i: (i, 0, 0))
    return pl.pallas_call(
        functools.partial(_attn_sample_kernel, seq=seq),
        grid=(rows // ROWS,),
        in_specs=[tile, _const_spec((1, D_MODEL)), _const_spec((D_MODEL, ATTN_COLS)),
                  _const_spec((A_WIDTH, D_MODEL)), _smem_spec(), _smem_spec(), cache, cache],
        out_specs=[tile, cache, cache],
        out_shape=[jax.ShapeDtypeStruct((rows, D_MODEL), F32),
                   jax.ShapeDtypeStruct(kc.shape, F32), jax.ShapeDtypeStruct(vc.shape, F32)],
        scratch_shapes=[pltpu.VMEM((ROWS, A_WIDTH), F32), pltpu.VMEM((ROWS, KV_WIDTH), F32),
                        pltpu.VMEM((ROWS, KV_WIDTH), F32), pltpu.VMEM((ROWS, A_WIDTH), F32),
                        pltpu.VMEM((A_HEADS, WINDOW, WINDOW), F32), pltpu.VMEM((A_HEADS, WINDOW, WINDOW), F32)],
        compiler_params=_params(1),
        name="attn_sample",
    )(x2d, g, wa, woa, rel_bias, sinks, kc, vc)


def _rwkv_scratch(n_extra):
    wide = pltpu.VMEM((ROWS, B_WIDTH), F32)
    return [wide] * (N_PREP + n_extra + 1) + [pltpu.VMEM((ROWS, 2 * B_WIDTH), F32),
                                              pltpu.VMEM((N_PAIRS, ROWS, 2 * ROWS), BF16),
                                              pltpu.VMEM((N_PAIRS, ROWS, 4 * ROWS), BF16)]


def _rwkv_weight_specs():
    row = _const_spec((1, B_WIDTH))
    return [_const_spec((1, D_MODEL)), _const_spec((D_MODEL, RWKV_COLS)), _const_spec((B_WIDTH, D_MODEL)),
            _const_spec((1, SHIFT_W)), row, _const_spec((LANES, 2 * B_WIDTH)), row, row, row, row, row, row]


def _rwkv_prompt(x, weights):
    bsz, t, _ = x.shape
    nb = t // ROWS
    return pl.pallas_call(
        _rwkv_prompt_kernel,
        grid=(bsz, nb + 1),
        in_specs=[pl.BlockSpec((1, ROWS, D_MODEL), lambda b, i: (b, jnp.minimum(i, nb - 1), 0))]
        + _rwkv_weight_specs(),
        out_specs=[pl.BlockSpec((1, ROWS, D_MODEL), lambda b, i: (b, jnp.maximum(i - 1, 0), 0)),
                   pl.BlockSpec((1, N_PAIRS, HEAD_DIM, LANES), lambda b, i: (b, 0, 0, 0)),
                   pl.BlockSpec((1, 1, SHIFT_W), lambda b, i: (b, 0, 0))],
        out_shape=[jax.ShapeDtypeStruct((bsz, t, D_MODEL), F32),
                   jax.ShapeDtypeStruct((bsz, N_PAIRS, HEAD_DIM, LANES), F32),
                   jax.ShapeDtypeStruct((bsz, 1, SHIFT_W), F32)],
        scratch_shapes=_rwkv_scratch(len(LATE_NAMES)) + [pltpu.VMEM((N_PAIRS, LANES, LANES), F32),
                                                         pltpu.VMEM((ROWS, SHIFT_W), F32),
                                                         pltpu.VMEM((1, SHIFT_W), F32)],
        compiler_params=_params(2),
        name="rwkv_prompt",
    )(x, *weights)


def _rwkv_sample(x2d, seq, weights, s_nat, shift_rows):
    rows = x2d.shape[0]
    nseq = ROWS // seq
    tile = pl.BlockSpec((ROWS, D_MODEL), lambda i: (i, 0))
    state = pl.BlockSpec((nseq, N_PAIRS, HEAD_DIM, LANES), lambda i: (i, 0, 0, 0))
    shift = pl.BlockSpec((ROWS, SHIFT_W), lambda i: (i, 0))
    wide = pltpu.VMEM((ROWS, B_WIDTH), F32)
    return pl.pallas_call(
        functools.partial(_rwkv_sample_kernel, seq=seq),
        grid=(rows // ROWS,),
        in_specs=[tile] + _rwkv_weight_specs() + [state, shift],
        out_specs=[tile, state, shift],
        out_shape=[jax.ShapeDtypeStruct((rows, D_MODEL), F32),
                   jax.ShapeDtypeStruct(s_nat.shape, F32),
                   jax.ShapeDtypeStruct((rows, SHIFT_W), F32)],
        scratch_shapes=_rwkv_scratch(0) + [wide, wide, pltpu.VMEM((N_PAIRS, ROWS, 2 * ROWS), BF16),
                                           pltpu.VMEM((N_PAIRS, 2 * ROWS, LANES), BF16)],
        compiler_params=_params(1),
        name="rwkv_sample",
    )(x2d, *weights, s_nat, shift_rows)


def _merge(x2d, ya, yb, g, wm, wo, fg, final, name):
    rows = x2d.shape[0]
    tm = 256
    tile = pl.BlockSpec((tm, D_MODEL), lambda i: (i, 0))
    return pl.pallas_call(
        functools.partial(_merge_kernel, final=final),
        grid=(rows // tm,),
        in_specs=[tile, tile, tile, _const_spec((1, D_MODEL)), _const_spec((D_MODEL, MERGE_COLS)),
                  _const_spec((D_MODEL, D_MODEL)), _const_spec((1, D_MODEL))],
        out_specs=tile,
        out_shape=jax.ShapeDtypeStruct((rows, D_MODEL), F32),
        compiler_params=_params(1),
        name=name,
    )(x2d, ya, yb, g, wm, wo, fg)


def _pairs_from_heads(s):
    b = s.shape[0]
    return s.reshape(b, N_PAIRS, 2, HEAD_DIM, HEAD_DIM).transpose(0, 1, 3, 2, 4).reshape(b, N_PAIRS, HEAD_DIM, LANES)


def _heads_from_pairs(s):
    b = s.shape[0]
    return s.reshape(b, N_PAIRS, HEAD_DIM, 2, HEAD_DIM).transpose(0, 1, 3, 2, 4).reshape(b, B_HEADS, HEAD_DIM, HEAD_DIM)


def kernel(x_prompt, x_sample, cache_k_win, cache_v_win, state_wkv, state_shift, rel_bias, norm_g, w_in, attn_sinks, shift_mu, rwkv_w0, rwkv_w2, rwkv_a0, rwkv_a2, rwkv_k_k, rwkv_k_a, rwkv_r_k, lnx_g, lnx_b, w_out_a, w_out_b, w_o, final_g):
    depth = w_in.shape[0]
    bsz, t, _ = x_prompt.shape
    dbsz, dseq, _ = x_sample.shape
    hp = x_prompt
    hs = x_sample.reshape(dbsz * dseq, D_MODEL)
    fg = final_g.reshape(1, D_MODEL)
    outs = [[] for _ in range(8)]
    for l in range(depth):
        g = norm_g[l].reshape(1, D_MODEL)
        w = w_in[l]
        wa = _bf(w[:, :ATTN_COLS])
        wr = _bf(w[:, ATTN_COLS:ATTN_COLS + RWKV_COLS])
        wm = _bf(w[:, ATTN_COLS + RWKV_COLS:])
        woa, wob, wo = _bf(w_out_a[l]), _bf(w_out_b[l]), _bf(w_o[l])
        zeros = jnp.zeros((DECAY_LORA, B_WIDTH), F32)
        lora = _bf(jnp.concatenate([jnp.concatenate([rwkv_w2[l], zeros], axis=1),
                                    jnp.concatenate([zeros, rwkv_a2[l]], axis=1)], axis=0))
        rw = [g, wr, wob, shift_mu[l].reshape(1, SHIFT_W), rwkv_w0[l].reshape(1, B_WIDTH), lora,
              rwkv_a0[l].reshape(1, B_WIDTH), rwkv_k_k[l].reshape(1, B_WIDTH), rwkv_k_a[l].reshape(1, B_WIDTH),
              rwkv_r_k[l].reshape(1, B_WIDTH), lnx_g[l].reshape(1, B_WIDTH), lnx_b[l].reshape(1, B_WIDTH)]
        sinks = attn_sinks[l]

        ya_p, k1, v1 = _attention_prompt(hp, g, wa, woa, rel_bias, sinks)
        yb_p, s1, t1 = _rwkv_prompt(hp, rw)
        hp = _merge(hp.reshape(bsz * t, D_MODEL), ya_p.reshape(bsz * t, D_MODEL), yb_p.reshape(bsz * t, D_MODEL),
                    g, wm, wo, fg, l == depth - 1, "merge_prompt").reshape(bsz, t, D_MODEL)

        kc = cache_k_win[l].reshape(dbsz, WINDOW, KV_WIDTH)
        vc = cache_v_win[l].reshape(dbsz, WINDOW, KV_WIDTH)
        ya_s, k2, v2 = _attention_sample(hs, dseq, g, wa, woa, rel_bias, sinks, kc, vc)
        shift_rows = jnp.repeat(state_shift[l], dseq, axis=0)
        yb_s, s2, ps_s = _rwkv_sample(hs, dseq, rw, _pairs_from_heads(state_wkv[l]), shift_rows)
        hs = _merge(hs, ya_s, yb_s, g, wm, wo, fg, l == depth - 1, "merge_sample")

        outs[0].append(k1.reshape(bsz, WINDOW, A_KV_HEADS, HEAD_DIM))
        outs[1].append(v1.reshape(bsz, WINDOW, A_KV_HEADS, HEAD_DIM))
        outs[2].append(_heads_from_pairs(s1))
        outs[3].append(t1.reshape(bsz, SHIFT_W))
        outs[4].append(k2.reshape(dbsz, WINDOW, A_KV_HEADS, HEAD_DIM))
        outs[5].append(v2.reshape(dbsz, WINDOW, A_KV_HEADS, HEAD_DIM))
        outs[6].append(_heads_from_pairs(s2))
        outs[7].append(ps_s.reshape(dbsz, dseq, SHIFT_W)[:, -1])
    y_prompt = hp
    y_sample = hs.reshape(dbsz, dseq, D_MODEL)
    return (y_prompt, y_sample) + tuple(jnp.stack(o) for o in outs)
```

```python
import functools
import math

import numpy as np
import jax
import jax.numpy as jnp
from jax import lax
from jax.experimental import pallas as pl
from jax.experimental.pallas import tpu as pltpu

D_MODEL = 1024
HEAD_DIM = 64
A_HEADS = 16
A_KV_HEADS = 4
A_WIDTH = A_HEADS * HEAD_DIM
KV_WIDTH = A_KV_HEADS * HEAD_DIM
WINDOW = 128
ATTN_SCALE = HEAD_DIM ** -0.5
N_BUCKETS = 32
MAX_DISTANCE = 128
B_HEADS = 16
B_WIDTH = B_HEADS * HEAD_DIM
DECAY_LORA = 64
A_LORA = 64
SHIFT_W = 3 * B_WIDTH + DECAY_LORA + A_LORA
GN_EPS = 64e-5
NORM_EPS = 1e-6

ATTN_COLS = 2 * A_WIDTH + 2 * KV_WIDTH
RWKV_COLS = SHIFT_W + B_WIDTH
MERGE_COLS = 2 * D_MODEL

ROWS = 128
LANES = 128
N_PAIRS = B_HEADS // 2
SEQ_UNROLL = 4
NEG = -1e30
VMEM_LIMIT = 56 * 1024 * 1024

F32 = jnp.float32
BF16 = jnp.bfloat16


def _bucket_ranges():
    d = np.arange(0, WINDOW + 1)
    max_exact = N_BUCKETS // 2
    df = np.maximum(d, 1).astype(np.float32)
    large = max_exact + (np.log(df / np.float32(max_exact)) / np.float32(math.log(MAX_DISTANCE / max_exact))
                         * np.float32(N_BUCKETS - max_exact)).astype(np.int32)
    large = np.minimum(large, N_BUCKETS - 1)
    bucket = np.where(d < max_exact, d, large)
    out = []
    for b in range(N_BUCKETS):
        idx = np.nonzero(bucket == b)[0]
        if idx.size:
            assert idx[-1] - idx[0] + 1 == idx.size
            out.append((b, int(idx[0]), int(idx[-1])))
    return tuple(out)


_BUCKET_RANGES = _bucket_ranges()


def _bf(x):
    return x.astype(BF16)


def _mm(a, b):
    return jnp.dot(_bf(a), _bf(b), preferred_element_type=F32)


def _mm_nt(a, b):
    return lax.dot_general(_bf(a), _bf(b), (((1,), (1,)), ((), ())), preferred_element_type=F32)


def _rms(x, g):
    ms = jnp.mean(x * x, axis=-1, keepdims=True)
    return x * lax.rsqrt(ms + NORM_EPS) * g


def _sigmoid(x):
    return 1.0 / (1.0 + jnp.exp(-x))


def _silu(x):
    return x * _sigmoid(x)


def _init_bias(relb_ref, bprev_ref, bcur_ref):
    qi = lax.broadcasted_iota(jnp.int32, (WINDOW, WINDOW), 0)
    kj = lax.broadcasted_iota(jnp.int32, (WINDOW, WINDOW), 1)
    delta = qi - kj

    def body(h, carry):
        bp = jnp.full((WINDOW, WINDOW), NEG, F32)
        bc = jnp.full((WINDOW, WINDOW), NEG, F32)
        for (b, lo, hi) in _BUCKET_RANGES:
            val = relb_ref[b, h]
            bp = jnp.where((delta >= lo - WINDOW) & (delta <= hi - WINDOW), val, bp)
            bc = jnp.where((delta >= lo) & (delta <= hi), val, bc)
        bprev_ref[h] = bp
        bcur_ref[h] = bc
        return carry

    lax.fori_loop(0, A_HEADS, body, 0)


def _attn_core(q, kp, vp, kc, vc, bprev_ref, bcur_ref, sinks_ref, tq, first):
    group = A_HEADS // A_KV_HEADS
    lo_half = lax.broadcasted_iota(jnp.int32, (1, LANES), 1) < HEAD_DIM
    hi_half = jnp.logical_not(lo_half)
    rowi = lax.broadcasted_iota(jnp.int32, (group * tq, 1), 0)
    sps, scs, sinks = [], [], []
    for kvh in range(A_KV_HEADS):
        i, c = divmod(kvh, 2)
        sl = slice(LANES * i, LANES * (i + 1))
        rows = []
        for g in range(group):
            hq = group * kvh + g
            qs = q[:, LANES * (hq // 2):LANES * (hq // 2 + 1)]
            qm = jnp.where(lo_half if hq % 2 == 0 else hi_half, qs, 0.0)
            if hq % 2 != c:
                qm = pltpu.roll(qm, HEAD_DIM, 1)
            rows.append(qm)
        lhs = _bf(jnp.concatenate(rows, axis=0))
        bp = bprev_ref[group * kvh:group * (kvh + 1), 0:tq, :].reshape(group * tq, WINDOW)
        bc = bcur_ref[group * kvh:group * (kvh + 1), 0:tq, :].reshape(group * tq, WINDOW)
        sp = _mm_nt(lhs, kp[:, sl]) + bp
        if first is not None:
            sp = jnp.where(first, NEG, sp)
        sps.append(sp)
        scs.append(_mm_nt(lhs, kc[:, sl]) + bc)
        sink = sinks_ref[group * kvh + group - 1]
        for g in range(group - 2, -1, -1):
            sink = jnp.where(rowi < (g + 1) * tq, sinks_ref[group * kvh + g], sink)
        sinks.append(sink)
    ms = [jnp.maximum(jnp.max(jnp.maximum(sp, sc), axis=-1, keepdims=True), sink)
          for sp, sc, sink in zip(sps, scs, sinks)]
    pps = [jnp.exp(sp - m) for sp, m in zip(sps, ms)]
    pcs = [jnp.exp(sc - m) for sc, m in zip(scs, ms)]
    outs = []
    for kvh in range(A_KV_HEADS):
        i, c = divmod(kvh, 2)
        sl = slice(LANES * i, LANES * (i + 1))
        own = lo_half if c == 0 else hi_half
        o = _mm(pps[kvh], jnp.where(own, vp[:, sl], 1.0)) + _mm(pcs[kvh], jnp.where(own, vc[:, sl], 1.0))
        den = pltpu.roll(o, HEAD_DIM, 1) + jnp.exp(sinks[kvh] - ms[kvh])
        outs.append(o / den)
    pieces = [None] * A_HEADS
    for kvh in range(A_KV_HEADS):
        for g in range(group):
            hq = group * kvh + g
            og = outs[kvh][g * tq:(g + 1) * tq]
            if hq % 2 != kvh % 2:
                og = pltpu.roll(og, HEAD_DIM, 1)
            pieces[hq] = og
    slabs = [jnp.where(lo_half, pieces[2 * s], pieces[2 * s + 1]) for s in range(A_HEADS // 2)]
    return jnp.concatenate(slabs, axis=1)


def _attn_project(x, g_ref, wa_ref):
    h = _rms(x, g_ref[...])
    proj = _mm(h, wa_ref[...])
    q = proj[:, :A_WIDTH] * ATTN_SCALE
    k = proj[:, A_WIDTH:A_WIDTH + KV_WIDTH]
    v = proj[:, A_WIDTH + KV_WIDTH:A_WIDTH + 2 * KV_WIDTH]
    ga = proj[:, A_WIDTH + 2 * KV_WIDTH:ATTN_COLS]
    ma = proj[:, ATTN_COLS:ATTN_COLS + D_MODEL]
    mb = proj[:, ATTN_COLS + D_MODEL:]
    return q, k, v, ga, ma, mb


def _merge_tail(x, o, ga, ma, mb, yb, woa_ref, wo_ref, fg_ref, final):
    ya = _mm(o * _silu(ga), woa_ref[...])
    merged = _sigmoid(ma) * ya + _sigmoid(mb) * yb
    out = x + _mm(merged, wo_ref[...])
    return _rms(out, fg_ref[...]) if final else out


def _attn_prompt_kernel(x_ref, yb_ref, g_ref, wa_ref, woa_ref, wo_ref, fg_ref, relb_ref, sinks_ref,
                        out_ref, kw_ref, vw_ref, kprev, vprev, bprev, bcur, *, final):
    b = pl.program_id(0)
    n = pl.program_id(1)

    @pl.when((b == 0) & (n == 0))
    def _():
        _init_bias(relb_ref, bprev, bcur)

    @pl.when(n == 0)
    def _():
        kprev[...] = jnp.zeros_like(kprev)
        vprev[...] = jnp.zeros_like(vprev)

    x = x_ref[0]
    q, k, v, ga, ma, mb = _attn_project(x, g_ref, wa_ref)
    o = _attn_core(q, kprev[...], vprev[...], k, v, bprev, bcur, sinks_ref, ROWS, n == 0)
    kprev[...] = k
    vprev[...] = v
    kw_ref[0] = k
    vw_ref[0] = v
    out_ref[0] = _merge_tail(x, o, ga, ma, mb, yb_ref[0], woa_ref, wo_ref, fg_ref, final)


def _attn_sample_kernel(x_ref, yb_ref, g_ref, wa_ref, woa_ref, wo_ref, fg_ref, relb_ref, sinks_ref, kc_ref, vc_ref,
                        out_ref, ko_ref, vo_ref, qbuf, kbuf, vbuf, obuf, bprev, bcur, *, seq, final):
    @pl.when(pl.program_id(0) == 0)
    def _():
        _init_bias(relb_ref, bprev, bcur)

    x = x_ref[...]
    q, k, v, ga, ma, mb = _attn_project(x, g_ref, wa_ref)
    qbuf[...] = q
    kbuf[...] = k
    vbuf[...] = v
    pad = jnp.zeros((WINDOW - seq, KV_WIDTH), F32)

    def body(j, carry):
        rows = pl.ds(pl.multiple_of(j * seq, seq), seq)
        kn = kbuf[rows, :]
        vn = vbuf[rows, :]
        kcj = kc_ref[j]
        vcj = vc_ref[j]
        o = _attn_core(qbuf[rows, :], kcj, vcj, jnp.concatenate([kn, pad], axis=0),
                       jnp.concatenate([vn, pad], axis=0), bprev, bcur, sinks_ref, seq, None)
        obuf[rows, :] = o
        ko_ref[j, pl.ds(0, WINDOW - seq), :] = kcj[seq:, :]
        ko_ref[j, pl.ds(WINDOW - seq, seq), :] = kn
        vo_ref[j, pl.ds(0, WINDOW - seq), :] = vcj[seq:, :]
        vo_ref[j, pl.ds(WINDOW - seq, seq), :] = vn
        return carry

    lax.fori_loop(0, ROWS // seq, body, 0, unroll=SEQ_UNROLL)
    out_ref[...] = _merge_tail(x, obuf[...], ga, ma, mb, yb_ref[...], woa_ref, wo_ref, fg_ref, final)


def _head_sums(x, ones_blocks, terms=2):
    outs = []
    width = ones_blocks.shape[0]
    for i in range(x.shape[1] // width):
        xs = x[:, width * i:width * (i + 1)]
        hi = _bf(xs)
        acc = jnp.dot(hi, ones_blocks, preferred_element_type=F32)
        if terms == 2:
            acc = acc + jnp.dot(_bf(xs - hi.astype(F32)), ones_blocks, preferred_element_type=F32)
        outs.append(acc)
    return jnp.concatenate(outs, axis=1)


def _seg_cumsum(x, seq):
    pos = lax.broadcasted_iota(jnp.int32, (x.shape[0], 1), 0) & (seq - 1)
    s = 1
    while s < seq:
        x = x + jnp.where(pos >= s, pltpu.roll(x, s, 0), 0.0)
        s *= 2
    return x


def _neumann_levels(ps, seq, out):
    n = ps[0].shape[0]
    ri = lax.broadcasted_iota(jnp.int32, (n, n), 0)
    ci = lax.broadcasted_iota(jnp.int32, (n, n), 1)
    eye = jnp.where(ri == ci, 1.0, 0.0)
    xs = [eye + p for p in ps]
    if seq > 2:
        pws = [_mm(p, p) for p in ps]
        yield
        span = 2
        while span < seq:
            last = 2 * span >= seq
            skip = span if span >= 16 else 0
            nxt_x, nxt_p = [], []
            for pw, x in zip(pws, xs):
                rhs = x if last else jnp.concatenate([pw, x], axis=1)
                res = _mm(pw[skip:], rhs)
                dx = res if last else res[:, n:]
                if skip:
                    nxt_x.append(jnp.concatenate([x[:skip], x[skip:] + dx], axis=0))
                else:
                    nxt_x.append(x + dx)
                if not last:
                    pn = res[:, :n]
                    nxt_p.append(jnp.concatenate([jnp.zeros((skip, n), F32), pn], axis=0) if skip else pn)
            xs, pws = nxt_x, nxt_p
            span *= 2
            yield
    out[:] = xs


PREP_NAMES = ("at", "rt", "bt", "kt", "bh", "kh", "v", "gt", "bonus", "sgb")
RWKV_WEIGHT_NAMES = ("g", "wr", "wob", "mu", "w0", "lora", "a0", "kk", "ka", "rk", "lng", "lnb")
EARLY_NAMES = ("at", "rt", "bt", "kt")
LATE_NAMES = tuple(n for n in PREP_NAMES if n not in EARLY_NAMES)
PREP_SLAB = 2 * LANES


def _ones_blocks():
    ri = lax.broadcasted_iota(jnp.int32, (2 * LANES, 2 * LANES), 0)
    ci = lax.broadcasted_iota(jnp.int32, (2 * LANES, 2 * LANES), 1)
    return jnp.where((ri >> 6) == (ci >> 6), 1.0, 0.0).astype(BF16)


def _rwkv_prep(x, first_ref, w, seq, out, ps_ref, lora_ref):
    nseq = ROWS // seq
    h = _rms(x, w["g"][...])
    proj = _mm(h, w["wr"][...])
    ps_ref[...] = proj[:, :SHIFT_W]
    out["sgb"][...] = _silu(proj[:, SHIFT_W:])
    pos = lax.broadcasted_iota(jnp.int32, (ROWS, 1), 0) & (seq - 1)

    def shifted(cols):
        cur = ps_ref[:, cols]
        prev = jnp.where(pos == 0, first_ref[:, cols], pltpu.roll(cur, 1, 0))
        return cur + (prev - cur) * w["mu"][:, cols]

    zl = shifted(slice(3 * B_WIDTH, SHIFT_W))
    lo_half = lax.broadcasted_iota(jnp.int32, (1, LANES), 1) < HEAD_DIM
    lora_ref[...] = _mm(jnp.where(lo_half, jnp.tanh(zl), zl), w["lora"][...])
    yield
    ones_blocks = _ones_blocks()
    for d in range(B_WIDTH // PREP_SLAB):
        cols = slice(PREP_SLAB * d, PREP_SLAB * (d + 1))

        def wcols(name):
            return w[name][:, cols]

        r = shifted(cols)
        k = shifted(slice(B_WIDTH + cols.start, B_WIDTH + cols.stop))
        v = shifted(slice(2 * B_WIDTH + cols.start, 2 * B_WIDTH + cols.stop))
        lw = -math.exp(-0.5) * _sigmoid(wcols("w0") + lora_ref[:, cols])
        asig = _sigmoid(wcols("a0") + lora_ref[:, slice(B_WIDTH + cols.start, B_WIDTH + cols.stop)])
        kkr = k * wcols("kk")
        kk = kkr / jnp.maximum(jnp.sqrt(_head_sums(kkr * kkr, ones_blocks)), 1e-12)
        kp = k * (1.0 + (asig - 1.0) * wcols("ka"))
        a_ = -kk
        b_ = kk * asig
        cum = _seg_cumsum(lw, seq)
        if nseq == 1:
            tot = jnp.broadcast_to(cum[ROWS - 1:ROWS, :], cum.shape)
        else:
            c3 = cum.reshape(nseq, seq, PREP_SLAB)
            tot = jnp.broadcast_to(c3[:, seq - 1:seq, :], c3.shape).reshape(cum.shape)
        g_inv = jnp.exp(-cum)
        g_last = jnp.exp(tot - cum)
        out["at"][:, cols] = a_ * jnp.exp(cum - lw)
        out["rt"][:, cols] = r * jnp.exp(cum)
        out["bt"][:, cols] = b_ * g_inv
        out["kt"][:, cols] = kp * g_inv
        out["bh"][:, cols] = b_ * g_last
        out["kh"][:, cols] = kp * g_last
        out["gt"][:, cols] = jnp.exp(tot)
        out["v"][:, cols] = v
        out["bonus"][:, cols] = _head_sums(r * kp * wcols("rk"), ones_blocks) * v
        yield


def _rwkv_core(buf, w, seq, av_ref, t_ref, arbk_ref, res, *, s_ref=None, si_ref=None, so_ref=None,
               x0_ref=None, y0_ref=None, uvt_ref=None, bkh_ref=None):
    nseq = ROWS // seq
    prompt = s_ref is not None
    lo_half = lax.broadcasted_iota(jnp.int32, (1, LANES), 1) < HEAD_DIM
    hi_half = jnp.logical_not(lo_half)
    rr = lax.broadcasted_iota(jnp.int32, (ROWS, ROWS), 0)
    cc = lax.broadcasted_iota(jnp.int32, (ROWS, ROWS), 1)
    shift = int(math.log2(seq))
    same = (rr >> shift) == (cc >> shift)
    strict = same & (rr > cc)
    incl = same & (rr >= cc)
    bd = (rr < HEAD_DIM) == (cc < HEAD_DIM)
    pair_slices = [slice(LANES * p, LANES * (p + 1)) for p in range(N_PAIRS)]
    at_ref, rt_ref, bt_ref, kt_ref, bh_ref, kh_ref, v_ref, gt_ref = (
        buf[n] for n in ("at", "rt", "bt", "kt", "bh", "kh", "v", "gt"))

    def halves(t):
        return jnp.concatenate([jnp.where(lo_half, t, 0.0), jnp.where(hi_half, t, 0.0)], axis=0)

    def blockdiag(s_nat):
        return jnp.concatenate([jnp.where(lo_half, s_nat, 0.0), jnp.where(lo_half, 0.0, s_nat)], axis=0)

    if prompt:
        s0s = [s_ref[p] for p in range(N_PAIRS)]
        x0s = [_mm_nt(at_ref[:, sl], s0) for sl, s0 in zip(pair_slices, s0s)]
        y0s = [_mm_nt(rt_ref[:, sl], s0) for sl, s0 in zip(pair_slices, s0s)]
    else:
        def read_body(j, carry):
            rows = pl.ds(pl.multiple_of(j * seq, seq), seq)
            for p, sl in enumerate(pair_slices):
                s0 = blockdiag(si_ref[j, p])
                arj = jnp.concatenate([at_ref[rows, sl], rt_ref[rows, sl]], axis=0)
                xy = _mm_nt(arj, s0)
                x0_ref[rows, sl] = xy[:seq]
                y0_ref[rows, sl] = xy[seq:]
            return carry
        lax.fori_loop(0, nseq, read_body, 0, unroll=SEQ_UNROLL)
        x0s = [x0_ref[:, sl] for sl in pair_slices]
        y0s = [y0_ref[:, sl] for sl in pair_slices]
    yield

    a_abs, a_aks = [], []
    for p, sl in enumerate(pair_slices):
        ar = jnp.concatenate([at_ref[:, sl], rt_ref[:, sl]], axis=0)
        bk = jnp.concatenate([bt_ref[:, sl], kt_ref[:, sl]], axis=0)
        gm = _mm_nt(halves(ar), bk)
        rbk = []
        for hh in range(2):
            base = 2 * ROWS * hh
            a_abs.append(jnp.where(strict, gm[base:base + ROWS, :ROWS], 0.0))
            a_aks.append(jnp.where(strict, gm[base:base + ROWS, ROWS:], 0.0))
            rbk.append(jnp.where(incl, gm[base + ROWS:base + 2 * ROWS, :ROWS], 0.0))
            rbk.append(jnp.where(incl, gm[base + ROWS:base + 2 * ROWS, ROWS:], 0.0))
        arbk_ref[p] = _bf(jnp.concatenate(rbk, axis=1))
        av_ref[:, sl] = _mm(jnp.concatenate(a_aks[2 * p:2 * p + 2], axis=1), halves(v_ref[:, sl]))
        if p % 4 == 3:
            yield
    ts = []
    yield from _neumann_levels(a_abs, seq, ts)
    for p in range(N_PAIRS):
        t_ref[p] = _bf(jnp.concatenate(ts[2 * p:2 * p + 2], axis=1))

    vss = [v_ref[:, sl] for sl in pair_slices]
    us = [jnp.dot(t_ref[p], _bf(halves(x0s[p] + av_ref[:, sl])), preferred_element_type=F32)
          for p, sl in enumerate(pair_slices)]
    yield
    ys = []
    for p in range(N_PAIRS):
        u, vs = us[p], vss[p]
        rhs = jnp.concatenate([jnp.where(lo_half, u, 0.0), jnp.where(lo_half, vs, 0.0),
                               jnp.where(hi_half, u, 0.0), jnp.where(hi_half, vs, 0.0)], axis=0)
        ys.append(y0s[p] + jnp.dot(arbk_ref[p], _bf(rhs), preferred_element_type=F32))
    yield

    uvts = [jnp.concatenate([us[p], vss[p]], axis=0).T for p in range(N_PAIRS)]
    bkhs = [jnp.concatenate([bh_ref[:, sl], kh_ref[:, sl]], axis=0) for sl in pair_slices]
    if prompt:
        s1s = [s0s[p] * gt_ref[0:1, sl] + jnp.where(bd, _mm(uvts[p], bkhs[p]), 0.0)
               for p, sl in enumerate(pair_slices)]
        for p in range(N_PAIRS):
            s_ref[p] = s1s[p]
        res["s1"] = s1s
    else:
        colseq = (lax.broadcasted_iota(jnp.int32, (1, 2 * ROWS), 1) & (ROWS - 1)) >> shift
        for p in range(N_PAIRS):
            uvt_ref[p] = _bf(uvts[p])
            bkh_ref[p] = _bf(bkhs[p])

        def upd_body(j, carry):
            row = pl.ds(pl.multiple_of(j * seq, seq), 1)
            for p, sl in enumerate(pair_slices):
                uvt = jnp.where(colseq == j, uvt_ref[p], jnp.zeros((), BF16))
                upd = jnp.dot(uvt, bkh_ref[p], preferred_element_type=F32)
                so_ref[j, p] = (si_ref[j, p] * gt_ref[row, sl]
                                + jnp.where(lo_half, upd[:HEAD_DIM], upd[HEAD_DIM:]))
            return carry
        lax.fori_loop(0, nseq, upd_body, 0, unroll=SEQ_UNROLL)
    yield

    ones_blocks = _ones_blocks()
    y = jnp.concatenate(ys, axis=1)
    mean = _head_sums(y, ones_blocks, terms=1) * (1.0 / HEAD_DIM)
    dlt = y - mean
    var = _head_sums(dlt * dlt, ones_blocks, terms=1) * (1.0 / HEAD_DIM)
    yn = dlt * lax.rsqrt(var + GN_EPS) * w["lng"][...] + w["lnb"][...]
    yo = (yn + buf["bonus"][...]) * buf["sgb"][...]
    res["yb"] = _mm(yo, w["wob"][...])


N_RWKV_W = len(RWKV_WEIGHT_NAMES)
N_PREP = len(PREP_NAMES)
PROMPT_ORDER = "CCCCCPCCPCCPCPCPCC"


def _rwkv_prompt_kernel(*refs):
    x_ref = refs[0]
    w = dict(zip(RWKV_WEIGHT_NAMES, refs[1:1 + N_RWKV_W]))
    yb_ref, so_ref, sh_ref = refs[1 + N_RWKV_W:4 + N_RWKV_W]
    scratch = refs[4 + N_RWKV_W:]
    cur = dict(zip(PREP_NAMES, scratch[:N_PREP]))
    nxt = dict(cur)
    nxt.update(zip(LATE_NAMES, scratch[N_PREP:N_PREP + len(LATE_NAMES)]))
    av_ref, lora_ref, t_ref, arbk_ref, s_ref, ps_ref, carry_ref = scratch[N_PREP + len(LATE_NAMES):]
    b = pl.program_id(0)
    i = pl.program_id(1)

    @pl.when((b == 0) & (i == 0))
    def _():
        for name in PREP_NAMES:
            cur[name][...] = jnp.zeros_like(cur[name])

    @pl.when(i == 0)
    def _():
        carry_ref[...] = jnp.zeros_like(carry_ref)

    @pl.when(i <= 1)
    def _():
        s_ref[...] = jnp.zeros_like(s_ref)

    res = {}
    prep = _rwkv_prep(x_ref[0], carry_ref, w, ROWS, nxt, ps_ref, lora_ref)
    core = _rwkv_core(cur, w, ROWS, av_ref, t_ref, arbk_ref, res, s_ref=s_ref)
    for kind in PROMPT_ORDER:
        next(prep if kind == "P" else core, None)
    for _ in prep:
        pass
    for _ in core:
        pass
    carry_ref[...] = ps_ref[ROWS - 1:ROWS, :]
    sh_ref[0] = ps_ref[ROWS - 1:ROWS, :]
    yb_ref[0] = res["yb"]

    @pl.when(i == pl.num_programs(1) - 1)
    def _():
        lo_half = lax.broadcasted_iota(jnp.int32, (1, LANES), 1) < HEAD_DIM
        for p in range(N_PAIRS):
            so_ref[0, p] = jnp.where(lo_half, res["s1"][p][:HEAD_DIM], res["s1"][p][HEAD_DIM:])

    for name in LATE_NAMES:
        cur[name][...] = nxt[name][...]


def _rwkv_sample_kernel(*refs, seq):
    x_ref = refs[0]
    w = dict(zip(RWKV_WEIGHT_NAMES, refs[1:1 + N_RWKV_W]))
    si_ref, shin_ref, yb_ref, so_ref, ps_ref = refs[1 + N_RWKV_W:6 + N_RWKV_W]
    scratch = refs[6 + N_RWKV_W:]
    buf = dict(zip(PREP_NAMES, scratch[:N_PREP]))
    av_ref, lora_ref, t_ref, arbk_ref, x0_ref, y0_ref, uvt_ref, bkh_ref = scratch[N_PREP:]
    res = {}
    for _ in _rwkv_prep(x_ref[...], shin_ref, w, seq, buf, ps_ref, lora_ref):
        pass
    for _ in _rwkv_core(buf, w, seq, av_ref, t_ref, arbk_ref, res, si_ref=si_ref, so_ref=so_ref,
                        x0_ref=x0_ref, y0_ref=y0_ref, uvt_ref=uvt_ref, bkh_ref=bkh_ref):
        pass
    yb_ref[...] = res["yb"]


def _const_spec(shape):
    nd = len(shape)
    return pl.BlockSpec(shape, lambda *_: (0,) * nd, pipeline_mode=pl.Buffered(1))


def _smem_spec():
    return pl.BlockSpec(memory_space=pltpu.SMEM)


def _params(n_axes):
    return pltpu.CompilerParams(dimension_semantics=("arbitrary",) * n_axes, vmem_limit_bytes=VMEM_LIMIT)


def _attn_weight_specs():
    return [_const_spec((1, D_MODEL)), _const_spec((D_MODEL, ATTN_COLS + MERGE_COLS)),
            _const_spec((A_WIDTH, D_MODEL)), _const_spec((D_MODEL, D_MODEL)), _const_spec((1, D_MODEL)),
            _smem_spec(), _smem_spec()]


def _attention_prompt(x, yb, weights, final):
    bsz, t, _ = x.shape
    nb = t // ROWS
    tile = pl.BlockSpec((1, ROWS, D_MODEL), lambda b, n: (b, n, 0))
    win = pl.BlockSpec((1, WINDOW, KV_WIDTH), lambda b, n: (b, 0, 0))
    return pl.pallas_call(
        functools.partial(_attn_prompt_kernel, final=final),
        grid=(bsz, nb),
        in_specs=[tile, tile] + _attn_weight_specs(),
        out_specs=[tile, win, win],
        out_shape=[jax.ShapeDtypeStruct((bsz, t, D_MODEL), F32),
                   jax.ShapeDtypeStruct((bsz, WINDOW, KV_WIDTH), F32),
                   jax.ShapeDtypeStruct((bsz, WINDOW, KV_WIDTH), F32)],
        scratch_shapes=[pltpu.VMEM((WINDOW, KV_WIDTH), F32), pltpu.VMEM((WINDOW, KV_WIDTH), F32),
                        pltpu.VMEM((A_HEADS, WINDOW, WINDOW), F32), pltpu.VMEM((A_HEADS, WINDOW, WINDOW), F32)],
        compiler_params=_params(2),
        name="attn_prompt",
    )(x, yb, *weights)


def _attention_sample(x2d, yb, seq, weights, kc, vc, final):
    rows = x2d.shape[0]
    nseq = ROWS // seq
    tile = pl.BlockSpec((ROWS, D_MODEL), lambda i: (i, 0))
    cache = pl.BlockSpec((nseq, WINDOW, KV_WIDTH), lambda i: (i, 0, 0))
    return pl.pallas_call(
        functools.partial(_attn_sample_kernel, seq=seq, final=final),
        grid=(rows // ROWS,),
        in_specs=[tile, tile] + _attn_weight_specs() + [cache, cache],
        out_specs=[tile, cache, cache],
        out_shape=[jax.ShapeDtypeStruct((rows, D_MODEL), F32),
                   jax.ShapeDtypeStruct(kc.shape, F32), jax.ShapeDtypeStruct(vc.shape, F32)],
        scratch_shapes=[pltpu.VMEM((ROWS, A_WIDTH), F32), pltpu.VMEM((ROWS, KV_WIDTH), F32),
                        pltpu.VMEM((ROWS, KV_WIDTH), F32), pltpu.VMEM((ROWS, A_WIDTH), F32),
                        pltpu.VMEM((A_HEADS, WINDOW, WINDOW), F32), pltpu.VMEM((A_HEADS, WINDOW, WINDOW), F32)],
        compiler_params=_params(1),
        name="attn_sample",
    )(x2d, yb, *weights, kc, vc)


def _rwkv_scratch(n_extra):
    wide = pltpu.VMEM((ROWS, B_WIDTH), F32)
    return [wide] * (N_PREP + n_extra + 1) + [pltpu.VMEM((ROWS, 2 * B_WIDTH), F32),
                                              pltpu.VMEM((N_PAIRS, ROWS, 2 * ROWS), BF16),
                                              pltpu.VMEM((N_PAIRS, ROWS, 4 * ROWS), BF16)]


def _rwkv_weight_specs():
    row = _const_spec((1, B_WIDTH))
    return [_const_spec((1, D_MODEL)), _const_spec((D_MODEL, RWKV_COLS)), _const_spec((B_WIDTH, D_MODEL)),
            _const_spec((1, SHIFT_W)), row, _const_spec((LANES, 2 * B_WIDTH)), row, row, row, row, row, row]


def _rwkv_prompt(x, weights):
    bsz, t, _ = x.shape
    nb = t // ROWS
    return pl.pallas_call(
        _rwkv_prompt_kernel,
        grid=(bsz, nb + 1),
        in_specs=[pl.BlockSpec((1, ROWS, D_MODEL), lambda b, i: (b, jnp.minimum(i, nb - 1), 0))]
        + _rwkv_weight_specs(),
        out_specs=[pl.BlockSpec((1, ROWS, D_MODEL), lambda b, i: (b, jnp.maximum(i - 1, 0), 0)),
                   pl.BlockSpec((1, N_PAIRS, HEAD_DIM, LANES), lambda b, i: (b, 0, 0, 0)),
                   pl.BlockSpec((1, 1, SHIFT_W), lambda b, i: (b, 0, 0))],
        out_shape=[jax.ShapeDtypeStruct((bsz, t, D_MODEL), F32),
                   jax.ShapeDtypeStruct((bsz, N_PAIRS, HEAD_DIM, LANES), F32),
                   jax.ShapeDtypeStruct((bsz, 1, SHIFT_W), F32)],
        scratch_shapes=_rwkv_scratch(len(LATE_NAMES)) + [pltpu.VMEM((N_PAIRS, LANES, LANES), F32),
                                                         pltpu.VMEM((ROWS, SHIFT_W), F32),
                                                         pltpu.VMEM((1, SHIFT_W), F32)],
        compiler_params=_params(2),
        name="rwkv_prompt",
    )(x, *weights)


def _rwkv_sample(x2d, seq, weights, s_nat, shift_rows):
    rows = x2d.shape[0]
    nseq = ROWS // seq
    tile = pl.BlockSpec((ROWS, D_MODEL), lambda i: (i, 0))
    state = pl.BlockSpec((nseq, N_PAIRS, HEAD_DIM, LANES), lambda i: (i, 0, 0, 0))
    shift = pl.BlockSpec((ROWS, SHIFT_W), lambda i: (i, 0))
    wide = pltpu.VMEM((ROWS, B_WIDTH), F32)
    return pl.pallas_call(
        functools.partial(_rwkv_sample_kernel, seq=seq),
        grid=(rows // ROWS,),
        in_specs=[tile] + _rwkv_weight_specs() + [state, shift],
        out_specs=[tile, state, shift],
        out_shape=[jax.ShapeDtypeStruct((rows, D_MODEL), F32),
                   jax.ShapeDtypeStruct(s_nat.shape, F32),
                   jax.ShapeDtypeStruct((rows, SHIFT_W), F32)],
        scratch_shapes=_rwkv_scratch(0) + [wide, wide, pltpu.VMEM((N_PAIRS, ROWS, 2 * ROWS), BF16),
                                           pltpu.VMEM((N_PAIRS, 2 * ROWS, LANES), BF16)],
        compiler_params=_params(1),
        name="rwkv_sample",
    )(x2d, *weights, s_nat, shift_rows)


def _pairs_from_heads(s):
    b = s.shape[0]
    return s.reshape(b, N_PAIRS, 2, HEAD_DIM, HEAD_DIM).transpose(0, 1, 3, 2, 4).reshape(b, N_PAIRS, HEAD_DIM, LANES)


def _heads_from_pairs(s):
    b = s.shape[0]
    return s.reshape(b, N_PAIRS, HEAD_DIM, 2, HEAD_DIM).transpose(0, 1, 3, 2, 4).reshape(b, B_HEADS, HEAD_DIM, HEAD_DIM)


def kernel(x_prompt, x_sample, cache_k_win, cache_v_win, state_wkv, state_shift, rel_bias, norm_g, w_in, attn_sinks, shift_mu, rwkv_w0, rwkv_w2, rwkv_a0, rwkv_a2, rwkv_k_k, rwkv_k_a, rwkv_r_k, lnx_g, lnx_b, w_out_a, w_out_b, w_o, final_g):
    depth = w_in.shape[0]
    bsz, t, _ = x_prompt.shape
    dbsz, dseq, _ = x_sample.shape
    hp = x_prompt
    hs = x_sample.reshape(dbsz * dseq, D_MODEL)
    fg = final_g.reshape(1, D_MODEL)
    outs = [[] for _ in range(8)]
    for l in range(depth):
        g = norm_g[l].reshape(1, D_MODEL)
        w = w_in[l]
        wa = _bf(jnp.concatenate([w[:, :ATTN_COLS], w[:, ATTN_COLS + RWKV_COLS:]], axis=1))
        wr = _bf(w[:, ATTN_COLS:ATTN_COLS + RWKV_COLS])
        woa, wob, wo = _bf(w_out_a[l]), _bf(w_out_b[l]), _bf(w_o[l])
        zeros = jnp.zeros((DECAY_LORA, B_WIDTH), F32)
        lora = _bf(jnp.concatenate([jnp.concatenate([rwkv_w2[l], zeros], axis=1),
                                    jnp.concatenate([zeros, rwkv_a2[l]], axis=1)], axis=0))
        rw = [g, wr, wob, shift_mu[l].reshape(1, SHIFT_W), rwkv_w0[l].reshape(1, B_WIDTH), lora,
              rwkv_a0[l].reshape(1, B_WIDTH), rwkv_k_k[l].reshape(1, B_WIDTH), rwkv_k_a[l].reshape(1, B_WIDTH),
              rwkv_r_k[l].reshape(1, B_WIDTH), lnx_g[l].reshape(1, B_WIDTH), lnx_b[l].reshape(1, B_WIDTH)]
        aw = [g, wa, woa, wo, fg, rel_bias, attn_sinks[l]]
        final = l == depth - 1

        yb_p, s1, t1 = _rwkv_prompt(hp, rw)
        hp, k1, v1 = _attention_prompt(hp, yb_p, aw, final)

        kc = cache_k_win[l].reshape(dbsz, WINDOW, KV_WIDTH)
        vc = cache_v_win[l].reshape(dbsz, WINDOW, KV_WIDTH)
        shift_rows = jnp.repeat(state_shift[l], dseq, axis=0)
        yb_s, s2, ps_s = _rwkv_sample(hs, dseq, rw, _pairs_from_heads(state_wkv[l]), shift_rows)
        hs, k2, v2 = _attention_sample(hs, yb_s, dseq, aw, kc, vc, final)

        outs[0].append(k1.reshape(bsz, WINDOW, A_KV_HEADS, HEAD_DIM))
        outs[1].append(v1.reshape(bsz, WINDOW, A_KV_HEADS, HEAD_DIM))
        outs[2].append(_heads_from_pairs(s1))
        outs[3].append(t1.reshape(bsz, SHIFT_W))
        outs[4].append(k2.reshape(dbsz, WINDOW, A_KV_HEADS, HEAD_DIM))
        outs[5].append(v2.reshape(dbsz, WINDOW, A_KV_HEADS, HEAD_DIM))
        outs[6].append(_heads_from_pairs(s2))
        outs[7].append(ps_s.reshape(dbsz, dseq, SHIFT_W)[:, -1])
    y_prompt = hp
    y_sample = hs.reshape(dbsz, dseq, D_MODEL)
    return (y_prompt, y_sample) + tuple(jnp.stack(o) for o in outs)
```

```python
import functools
import math

import numpy as np
import jax
import jax.numpy as jnp
from jax import lax
from jax.experimental import pallas as pl
from jax.experimental.pallas import tpu as pltpu

D_MODEL = 1024
HEAD_DIM = 64
A_HEADS = 16
A_KV_HEADS = 4
A_WIDTH = A_HEADS * HEAD_DIM
KV_WIDTH = A_KV_HEADS * HEAD_DIM
WINDOW = 128
ATTN_SCALE = HEAD_DIM ** -0.5
N_BUCKETS = 32
MAX_DISTANCE = 128
B_HEADS = 16
B_WIDTH = B_HEADS * HEAD_DIM
DECAY_LORA = 64
A_LORA = 64
SHIFT_W = 3 * B_WIDTH + DECAY_LORA + A_LORA
GN_EPS = 64e-5
NORM_EPS = 1e-6

ATTN_COLS = 2 * A_WIDTH + 2 * KV_WIDTH
RWKV_COLS = SHIFT_W + B_WIDTH
MERGE_COLS = 2 * D_MODEL

ROWS = 128
LANES = 128
N_PAIRS = B_HEADS // 2
SEQ_UNROLL = 4
NEG = -1e30
VMEM_LIMIT = 56 * 1024 * 1024

F32 = jnp.float32
BF16 = jnp.bfloat16


def _bucket_ranges():
    d = np.arange(0, WINDOW + 1)
    max_exact = N_BUCKETS // 2
    df = np.maximum(d, 1).astype(np.float32)
    large = max_exact + (np.log(df / np.float32(max_exact)) / np.float32(math.log(MAX_DISTANCE / max_exact))
                         * np.float32(N_BUCKETS - max_exact)).astype(np.int32)
    large = np.minimum(large, N_BUCKETS - 1)
    bucket = np.where(d < max_exact, d, large)
    out = []
    for b in range(N_BUCKETS):
        idx = np.nonzero(bucket == b)[0]
        if idx.size:
            assert idx[-1] - idx[0] + 1 == idx.size
            out.append((b, int(idx[0]), int(idx[-1])))
    return tuple(out)


_BUCKET_RANGES = _bucket_ranges()


def _bf(x):
    return x.astype(BF16)


def _mm(a, b):
    return jnp.dot(_bf(a), _bf(b), preferred_element_type=F32)


def _mm_nt(a, b):
    return lax.dot_general(_bf(a), _bf(b), (((1,), (1,)), ((), ())), preferred_element_type=F32)


def _rms(x, g):
    ms = jnp.mean(x * x, axis=-1, keepdims=True)
    return x * lax.rsqrt(ms + NORM_EPS) * g


def _sigmoid(x):
    return 1.0 / (1.0 + jnp.exp(-x))


def _silu(x):
    return x * _sigmoid(x)


def _init_bias(relb_ref, bprev_ref, bcur_ref):
    qi = lax.broadcasted_iota(jnp.int32, (WINDOW, WINDOW), 0)
    kj = lax.broadcasted_iota(jnp.int32, (WINDOW, WINDOW), 1)
    delta = qi - kj

    def body(h, carry):
        bp = jnp.full((WINDOW, WINDOW), NEG, F32)
        bc = jnp.full((WINDOW, WINDOW), NEG, F32)
        for (b, lo, hi) in _BUCKET_RANGES:
            val = relb_ref[b, h]
            bp = jnp.where((delta >= lo - WINDOW) & (delta <= hi - WINDOW), val, bp)
            bc = jnp.where((delta >= lo) & (delta <= hi), val, bc)
        bprev_ref[h] = bp
        bcur_ref[h] = bc
        return carry

    lax.fori_loop(0, A_HEADS, body, 0)


def _attn_core(q, kp, vp, kc, vc, bprev_ref, bcur_ref, sinks_ref, tq, first):
    group = A_HEADS // A_KV_HEADS
    lo_half = lax.broadcasted_iota(jnp.int32, (1, LANES), 1) < HEAD_DIM
    hi_half = jnp.logical_not(lo_half)
    rowi = lax.broadcasted_iota(jnp.int32, (group * tq, 1), 0)
    sps, scs, sinks = [], [], []
    for kvh in range(A_KV_HEADS):
        i, c = divmod(kvh, 2)
        sl = slice(LANES * i, LANES * (i + 1))
        rows = []
        for g in range(group):
            hq = group * kvh + g
            qs = q[:, LANES * (hq // 2):LANES * (hq // 2 + 1)]
            qm = jnp.where(lo_half if hq % 2 == 0 else hi_half, qs, 0.0)
            if hq % 2 != c:
                qm = pltpu.roll(qm, HEAD_DIM, 1)
            rows.append(qm)
        lhs = _bf(jnp.concatenate(rows, axis=0))
        bp = bprev_ref[group * kvh:group * (kvh + 1), 0:tq, :].reshape(group * tq, WINDOW)
        bc = bcur_ref[group * kvh:group * (kvh + 1), 0:tq, :].reshape(group * tq, WINDOW)
        sp = _mm_nt(lhs, kp[:, sl]) + bp
        if first is not None:
            sp = jnp.where(first, NEG, sp)
        sps.append(sp)
        scs.append(_mm_nt(lhs, kc[:, sl]) + bc)
        sink = sinks_ref[group * kvh + group - 1]
        for g in range(group - 2, -1, -1):
            sink = jnp.where(rowi < (g + 1) * tq, sinks_ref[group * kvh + g], sink)
        sinks.append(sink)
    ms = [jnp.maximum(jnp.max(jnp.maximum(sp, sc), axis=-1, keepdims=True), sink)
          for sp, sc, sink in zip(sps, scs, sinks)]
    pps = [jnp.exp(sp - m) for sp, m in zip(sps, ms)]
    pcs = [jnp.exp(sc - m) for sc, m in zip(scs, ms)]
    outs = []
    for kvh in range(A_KV_HEADS):
        i, c = divmod(kvh, 2)
        sl = slice(LANES * i, LANES * (i + 1))
        own = lo_half if c == 0 else hi_half
        o = _mm(pps[kvh], jnp.where(own, vp[:, sl], 1.0)) + _mm(pcs[kvh], jnp.where(own, vc[:, sl], 1.0))
        den = pltpu.roll(o, HEAD_DIM, 1) + jnp.exp(sinks[kvh] - ms[kvh])
        outs.append(o / den)
    pieces = [None] * A_HEADS
    for kvh in range(A_KV_HEADS):
        for g in range(group):
            hq = group * kvh + g
            og = outs[kvh][g * tq:(g + 1) * tq]
            if hq % 2 != kvh % 2:
                og = pltpu.roll(og, HEAD_DIM, 1)
            pieces[hq] = og
    slabs = [jnp.where(lo_half, pieces[2 * s], pieces[2 * s + 1]) for s in range(A_HEADS // 2)]
    return jnp.concatenate(slabs, axis=1)


def _attn_project(x, g_ref, wa_ref):
    h = _rms(x, g_ref[...])
    proj = _mm(h, wa_ref[...])
    q = proj[:, :A_WIDTH] * ATTN_SCALE
    k = proj[:, A_WIDTH:A_WIDTH + KV_WIDTH]
    v = proj[:, A_WIDTH + KV_WIDTH:A_WIDTH + 2 * KV_WIDTH]
    ga = proj[:, A_WIDTH + 2 * KV_WIDTH:ATTN_COLS]
    ma = proj[:, ATTN_COLS:ATTN_COLS + D_MODEL]
    mb = proj[:, ATTN_COLS + D_MODEL:]
    return q, k, v, ga, ma, mb


def _merge_tail(x, o, ga, ma, mb, yb, woa_ref, wo_ref, fg_ref, final):
    ya = _mm(o * _silu(ga), woa_ref[...])
    merged = _sigmoid(ma) * ya + _sigmoid(mb) * yb
    out = x + _mm(merged, wo_ref[...])
    return _rms(out, fg_ref[...]) if final else out


def _attn_prompt_kernel(x_ref, yb_ref, g_ref, wa_ref, woa_ref, wo_ref, fg_ref, relb_ref, sinks_ref,
                        out_ref, kw_ref, vw_ref, kprev, vprev, bprev, bcur, *, final):
    b = pl.program_id(0)
    n = pl.program_id(1)

    @pl.when((b == 0) & (n == 0))
    def _():
        _init_bias(relb_ref, bprev, bcur)

    @pl.when(n == 0)
    def _():
        kprev[...] = jnp.zeros_like(kprev)
        vprev[...] = jnp.zeros_like(vprev)

    x = x_ref[0]
    q, k, v, ga, ma, mb = _attn_project(x, g_ref, wa_ref)
    o = _attn_core(q, kprev[...], vprev[...], k, v, bprev, bcur, sinks_ref, ROWS, n == 0)
    kprev[...] = k
    vprev[...] = v
    kw_ref[0] = k
    vw_ref[0] = v
    out_ref[0] = _merge_tail(x, o, ga, ma, mb, yb_ref[0], woa_ref, wo_ref, fg_ref, final)


def _attn_sample_kernel(x_ref, yb_ref, g_ref, wa_ref, woa_ref, wo_ref, fg_ref, relb_ref, sinks_ref, kc_ref, vc_ref,
                        out_ref, ko_ref, vo_ref, qbuf, kbuf, vbuf, obuf, bprev, bcur, *, seq, final):
    @pl.when(pl.program_id(0) == 0)
    def _():
        _init_bias(relb_ref, bprev, bcur)

    x = x_ref[...]
    q, k, v, ga, ma, mb = _attn_project(x, g_ref, wa_ref)
    qbuf[...] = q
    kbuf[...] = k
    vbuf[...] = v
    pad = jnp.zeros((WINDOW - seq, KV_WIDTH), F32)

    def body(j, carry):
        rows = pl.ds(pl.multiple_of(j * seq, seq), seq)
        kn = kbuf[rows, :]
        vn = vbuf[rows, :]
        kcj = kc_ref[j]
        vcj = vc_ref[j]
        o = _attn_core(qbuf[rows, :], kcj, vcj, jnp.concatenate([kn, pad], axis=0),
                       jnp.concatenate([vn, pad], axis=0), bprev, bcur, sinks_ref, seq, None)
        obuf[rows, :] = o
        ko_ref[j, pl.ds(0, WINDOW - seq), :] = kcj[seq:, :]
        ko_ref[j, pl.ds(WINDOW - seq, seq), :] = kn
        vo_ref[j, pl.ds(0, WINDOW - seq), :] = vcj[seq:, :]
        vo_ref[j, pl.ds(WINDOW - seq, seq), :] = vn
        return carry

    lax.fori_loop(0, ROWS // seq, body, 0, unroll=SEQ_UNROLL)
    out_ref[...] = _merge_tail(x, obuf[...], ga, ma, mb, yb_ref[...], woa_ref, wo_ref, fg_ref, final)


def _head_sums(x, ones_blocks, terms=2):
    outs = []
    width = ones_blocks.shape[0]
    for i in range(x.shape[1] // width):
        xs = x[:, width * i:width * (i + 1)]
        hi = _bf(xs)
        acc = jnp.dot(hi, ones_blocks, preferred_element_type=F32)
        if terms == 2:
            acc = acc + jnp.dot(_bf(xs - hi.astype(F32)), ones_blocks, preferred_element_type=F32)
        outs.append(acc)
    return jnp.concatenate(outs, axis=1)


def _seg_cumsum(x, seq):
    pos = lax.broadcasted_iota(jnp.int32, (x.shape[0], 1), 0) & (seq - 1)
    s = 1
    while s < seq:
        if s % 8 == 0 and seq == x.shape[0]:
            x = jnp.concatenate([x[:s], x[s:] + x[:-s]], axis=0)
        else:
            x = x + jnp.where(pos >= s, pltpu.roll(x, s, 0), 0.0)
        s *= 2
    return x


def _neumann_levels(ps, seq, out):
    n = ps[0].shape[0]
    ri = lax.broadcasted_iota(jnp.int32, (n, n), 0)
    ci = lax.broadcasted_iota(jnp.int32, (n, n), 1)
    eye = jnp.where(ri == ci, 1.0, 0.0)
    xs = [eye + p for p in ps]
    if seq > 2:
        pws = [_mm(p, p) for p in ps]
        yield
        span = 2
        while span < seq:
            last = 2 * span >= seq
            skip = span if span >= 16 else 0
            nxt_x, nxt_p = [], []
            for pw, x in zip(pws, xs):
                rhs = x if last else jnp.concatenate([pw, x], axis=1)
                res = _mm(pw[skip:], rhs)
                dx = res if last else res[:, n:]
                if skip:
                    nxt_x.append(jnp.concatenate([x[:skip], x[skip:] + dx], axis=0))
                else:
                    nxt_x.append(x + dx)
                if not last:
                    pn = res[:, :n]
                    nxt_p.append(jnp.concatenate([jnp.zeros((skip, n), F32), pn], axis=0) if skip else pn)
            xs, pws = nxt_x, nxt_p
            span *= 2
            yield
    out[:] = xs


PREP_NAMES = ("at", "rt", "bkt", "v", "gt", "bonus", "sgb", "bkh")
RWKV_WEIGHT_NAMES = ("g", "wr", "wob", "mu", "w0", "lora", "a0", "kk", "ka", "rk", "lng", "lnb")
EARLY_NAMES = ("at", "rt", "bkt")
LATE_NAMES = tuple(n for n in PREP_NAMES if n not in EARLY_NAMES)
PREP_SLAB = 2 * LANES


def _ones_blocks():
    ri = lax.broadcasted_iota(jnp.int32, (2 * LANES, 2 * LANES), 0)
    ci = lax.broadcasted_iota(jnp.int32, (2 * LANES, 2 * LANES), 1)
    return jnp.where((ri >> 6) == (ci >> 6), 1.0, 0.0).astype(BF16)


def _rwkv_prep(x, first_ref, w, seq, out, ps_ref, lora_ref):
    nseq = ROWS // seq
    h = _rms(x, w["g"][...])
    proj = _mm(h, w["wr"][...])
    ps_ref[...] = proj[:, :SHIFT_W]
    out["sgb"][...] = _silu(proj[:, SHIFT_W:])
    pos = lax.broadcasted_iota(jnp.int32, (ROWS, 1), 0) & (seq - 1)

    def shifted(cols):
        cur = ps_ref[:, cols]
        prev = jnp.where(pos == 0, first_ref[:, cols], pltpu.roll(cur, 1, 0))
        return cur + (prev - cur) * w["mu"][:, cols]

    zl = shifted(slice(3 * B_WIDTH, SHIFT_W))
    lo_half = lax.broadcasted_iota(jnp.int32, (1, LANES), 1) < HEAD_DIM
    lora_ref[...] = _mm(jnp.where(lo_half, jnp.tanh(zl), zl), w["lora"][...])
    yield
    ones_blocks = _ones_blocks()
    for d in range(B_WIDTH // PREP_SLAB):
        cols = slice(PREP_SLAB * d, PREP_SLAB * (d + 1))

        def wcols(name):
            return w[name][:, cols]

        r = shifted(cols)
        k = shifted(slice(B_WIDTH + cols.start, B_WIDTH + cols.stop))
        v = shifted(slice(2 * B_WIDTH + cols.start, 2 * B_WIDTH + cols.stop))
        lw = -math.exp(-0.5) * _sigmoid(wcols("w0") + lora_ref[:, cols])
        asig = _sigmoid(wcols("a0") + lora_ref[:, slice(B_WIDTH + cols.start, B_WIDTH + cols.stop)])
        kkr = k * wcols("kk")
        kk = kkr * lax.rsqrt(jnp.maximum(_head_sums(kkr * kkr, ones_blocks), 1e-24))
        kp = k * (1.0 + (asig - 1.0) * wcols("ka"))
        a_ = -kk
        b_ = kk * asig
        cum = _seg_cumsum(lw, seq)
        if nseq == 1:
            tot = jnp.broadcast_to(cum[ROWS - 1:ROWS, :], cum.shape)
        else:
            c3 = cum.reshape(nseq, seq, PREP_SLAB)
            tot = jnp.broadcast_to(c3[:, seq - 1:seq, :], c3.shape).reshape(cum.shape)
        g_inv = jnp.exp(-cum)
        g_last = jnp.exp(tot - cum)
        out["at"][:, cols] = a_ * jnp.exp(cum - lw)
        out["rt"][:, cols] = r * jnp.exp(cum)
        bt = b_ * g_inv
        kt = kp * g_inv
        bh = b_ * g_last
        kh = kp * g_last
        for q in range(PREP_SLAB // LANES):
            ql = slice(LANES * q, LANES * (q + 1))
            pair = PREP_SLAB // LANES * d + q
            out["bkt"][pair] = _bf(jnp.concatenate([bt[:, ql], kt[:, ql]], axis=0).T)
            out["bkh"][pair] = _bf(jnp.concatenate([bh[:, ql], kh[:, ql]], axis=0))
        out["gt"][:, cols] = jnp.exp(tot)
        out["v"][:, cols] = v
        out["bonus"][:, cols] = _head_sums(r * kp * wcols("rk"), ones_blocks) * v
        yield


def _rwkv_core(buf, w, seq, av_ref, t_ref, arbk_ref, res, *, s_ref=None, si_ref=None, so_ref=None,
               x0_ref=None, y0_ref=None, uvt_ref=None):
    nseq = ROWS // seq
    prompt = s_ref is not None
    lo_half = lax.broadcasted_iota(jnp.int32, (1, LANES), 1) < HEAD_DIM
    hi_half = jnp.logical_not(lo_half)
    rr = lax.broadcasted_iota(jnp.int32, (ROWS, ROWS), 0)
    cc = lax.broadcasted_iota(jnp.int32, (ROWS, ROWS), 1)
    shift = int(math.log2(seq))
    same = (rr >> shift) == (cc >> shift)
    strict = same & (rr > cc)
    incl = same & (rr >= cc)
    bd = (rr < HEAD_DIM) == (cc < HEAD_DIM)
    pair_slices = [slice(LANES * p, LANES * (p + 1)) for p in range(N_PAIRS)]
    at_ref, rt_ref, bkt_ref, v_ref, gt_ref, bkh_ref = (buf[n] for n in ("at", "rt", "bkt", "v", "gt", "bkh"))

    def halves(t):
        return jnp.concatenate([jnp.where(lo_half, t, 0.0), jnp.where(hi_half, t, 0.0)], axis=0)

    def blockdiag(s_nat):
        return jnp.concatenate([jnp.where(lo_half, s_nat, 0.0), jnp.where(lo_half, 0.0, s_nat)], axis=0)

    if prompt:
        s0s = [s_ref[p] for p in range(N_PAIRS)]
        s0ts = [_bf(s0.T) for s0 in s0s]
        x0s = [_mm(at_ref[:, sl], s0t) for sl, s0t in zip(pair_slices, s0ts)]
        y0s = [_mm(rt_ref[:, sl], s0t) for sl, s0t in zip(pair_slices, s0ts)]
    else:
        def read_body(j, carry):
            rows = pl.ds(pl.multiple_of(j * seq, seq), seq)
            for p, sl in enumerate(pair_slices):
                s0 = blockdiag(si_ref[j, p])
                arj = jnp.concatenate([at_ref[rows, sl], rt_ref[rows, sl]], axis=0)
                xy = _mm_nt(arj, s0)
                x0_ref[rows, sl] = xy[:seq]
                y0_ref[rows, sl] = xy[seq:]
            return carry
        lax.fori_loop(0, nseq, read_body, 0, unroll=SEQ_UNROLL)
        x0s = [x0_ref[:, sl] for sl in pair_slices]
        y0s = [y0_ref[:, sl] for sl in pair_slices]
    yield

    a_abs, a_aks = [], []
    for p, sl in enumerate(pair_slices):
        rbk = []
        for hh in range(2):
            half = lo_half if hh == 0 else hi_half
            ga = _mm(jnp.where(half, at_ref[:, sl], 0.0), bkt_ref[p])
            gr = _mm(jnp.where(half, rt_ref[:, sl], 0.0), bkt_ref[p])
            a_abs.append(jnp.where(strict, ga[:, :ROWS], 0.0))
            a_aks.append(jnp.where(strict, ga[:, ROWS:], 0.0))
            rbk.append(jnp.where(incl, gr[:, :ROWS], 0.0))
            rbk.append(jnp.where(incl, gr[:, ROWS:], 0.0))
        arbk_ref[p] = _bf(jnp.concatenate(rbk, axis=1))
        av_ref[:, sl] = _mm(jnp.concatenate(a_aks[2 * p:2 * p + 2], axis=1), halves(v_ref[:, sl]))
        if p % 4 == 3:
            yield
    ts = []
    yield from _neumann_levels(a_abs, seq, ts)
    for p in range(N_PAIRS):
        t_ref[p] = _bf(jnp.concatenate(ts[2 * p:2 * p + 2], axis=1))

    vss = [v_ref[:, sl] for sl in pair_slices]
    us = [jnp.dot(t_ref[p], _bf(halves(x0s[p] + av_ref[:, sl])), preferred_element_type=F32)
          for p, sl in enumerate(pair_slices)]
    yield
    ys = []
    for p in range(N_PAIRS):
        u, vs = us[p], vss[p]
        rhs = jnp.concatenate([jnp.where(lo_half, u, 0.0), jnp.where(lo_half, vs, 0.0),
                               jnp.where(hi_half, u, 0.0), jnp.where(hi_half, vs, 0.0)], axis=0)
        ys.append(y0s[p] + jnp.dot(arbk_ref[p], _bf(rhs), preferred_element_type=F32))
    yield

    uvts = [jnp.concatenate([us[p], vss[p]], axis=0).T for p in range(N_PAIRS)]
    if prompt:
        s1s = [s0s[p] * gt_ref[0:1, sl] + jnp.where(bd, _mm(uvts[p], bkh_ref[p]), 0.0)
               for p, sl in enumerate(pair_slices)]
        for p in range(N_PAIRS):
            s_ref[p] = s1s[p]
        res["s1"] = s1s
    else:
        colseq = (lax.broadcasted_iota(jnp.int32, (1, 2 * ROWS), 1) & (ROWS - 1)) >> shift
        for p in range(N_PAIRS):
            uvt_ref[p] = _bf(uvts[p])

        def upd_body(j, carry):
            row = pl.ds(pl.multiple_of(j * seq, seq), 1)
            for p, sl in enumerate(pair_slices):
                uvt = jnp.where(colseq == j, uvt_ref[p], jnp.zeros((), BF16))
                upd = jnp.dot(uvt, bkh_ref[p], preferred_element_type=F32)
                so_ref[j, p] = (si_ref[j, p] * gt_ref[row, sl]
                                + jnp.where(lo_half, upd[:HEAD_DIM], upd[HEAD_DIM:]))
            return carry
        lax.fori_loop(0, nseq, upd_body, 0, unroll=SEQ_UNROLL)
    yield

    ones_blocks = _ones_blocks()
    y = jnp.concatenate(ys, axis=1)
    mean = _head_sums(y, ones_blocks, terms=1) * (1.0 / HEAD_DIM)
    dlt = y - mean
    var = _head_sums(dlt * dlt, ones_blocks, terms=1) * (1.0 / HEAD_DIM)
    yn = dlt * lax.rsqrt(var + GN_EPS) * w["lng"][...] + w["lnb"][...]
    yo = (yn + buf["bonus"][...]) * buf["sgb"][...]
    res["yb"] = _mm(yo, w["wob"][...])


N_RWKV_W = len(RWKV_WEIGHT_NAMES)
N_PREP = len(PREP_NAMES)
PROMPT_ORDER = "CCCCPCCPCCPCCPCPCC"


def _rwkv_prompt_kernel(*refs, tiles_per_seq):
    x_ref = refs[0]
    w = dict(zip(RWKV_WEIGHT_NAMES, refs[1:1 + N_RWKV_W]))
    yb_ref, so_ref, sh_ref = refs[1 + N_RWKV_W:4 + N_RWKV_W]
    scratch = refs[4 + N_RWKV_W:]
    cur = dict(zip(PREP_NAMES, scratch[:N_PREP]))
    nxt = dict(cur)
    nxt.update(zip(LATE_NAMES, scratch[N_PREP:N_PREP + len(LATE_NAMES)]))
    av_ref, lora_ref, t_ref, arbk_ref, s_ref, ps_ref, carry_ref = scratch[N_PREP + len(LATE_NAMES):]
    i = pl.program_id(0)
    n_prep = lax.rem(i, tiles_per_seq)

    @pl.when(i == 0)
    def _():
        for name in PREP_NAMES:
            cur[name][...] = jnp.zeros_like(cur[name])

    @pl.when(n_prep == 0)
    def _():
        carry_ref[...] = jnp.zeros_like(carry_ref)

    @pl.when((n_prep == 1) | (i == 0))
    def _():
        s_ref[...] = jnp.zeros_like(s_ref)

    res = {}
    prep = _rwkv_prep(x_ref[0], carry_ref, w, ROWS, nxt, ps_ref, lora_ref)
    core = _rwkv_core(cur, w, ROWS, av_ref, t_ref, arbk_ref, res, s_ref=s_ref)
    for kind in PROMPT_ORDER:
        next(prep if kind == "P" else core, None)
    for _ in prep:
        pass
    for _ in core:
        pass
    carry_ref[...] = ps_ref[ROWS - 1:ROWS, :]
    sh_ref[0] = ps_ref[ROWS - 1:ROWS, :]
    yb_ref[0] = res["yb"]

    @pl.when((n_prep == 0) & (i > 0))
    def _():
        lo_half = lax.broadcasted_iota(jnp.int32, (1, LANES), 1) < HEAD_DIM
        for p in range(N_PAIRS):
            so_ref[0, p] = jnp.where(lo_half, res["s1"][p][:HEAD_DIM], res["s1"][p][HEAD_DIM:])

    for name in LATE_NAMES:
        if name == "gt":
            cur[name][0:1, :] = nxt[name][0:1, :]
        else:
            cur[name][...] = nxt[name][...]


def _rwkv_sample_kernel(*refs, seq):
    x_ref = refs[0]
    w = dict(zip(RWKV_WEIGHT_NAMES, refs[1:1 + N_RWKV_W]))
    si_ref, shin_ref, yb_ref, so_ref, ps_ref = refs[1 + N_RWKV_W:6 + N_RWKV_W]
    scratch = refs[6 + N_RWKV_W:]
    buf = dict(zip(PREP_NAMES, scratch[:N_PREP]))
    av_ref, lora_ref, t_ref, arbk_ref, x0_ref, y0_ref, uvt_ref = scratch[N_PREP:]
    res = {}
    for _ in _rwkv_prep(x_ref[...], shin_ref, w, seq, buf, ps_ref, lora_ref):
        pass
    for _ in _rwkv_core(buf, w, seq, av_ref, t_ref, arbk_ref, res, si_ref=si_ref, so_ref=so_ref,
                        x0_ref=x0_ref, y0_ref=y0_ref, uvt_ref=uvt_ref):
        pass
    yb_ref[...] = res["yb"]


def _const_spec(shape):
    nd = len(shape)
    return pl.BlockSpec(shape, lambda *_: (0,) * nd, pipeline_mode=pl.Buffered(1))


def _smem_spec():
    return pl.BlockSpec(memory_space=pltpu.SMEM)


def _params(n_axes):
    return pltpu.CompilerParams(dimension_semantics=("arbitrary",) * n_axes, vmem_limit_bytes=VMEM_LIMIT)


def _attn_weight_specs():
    return [_const_spec((1, D_MODEL)), _const_spec((D_MODEL, ATTN_COLS + MERGE_COLS)),
            _const_spec((A_WIDTH, D_MODEL)), _const_spec((D_MODEL, D_MODEL)), _const_spec((1, D_MODEL)),
            _smem_spec(), _smem_spec()]


def _attention_prompt(x, yb, weights, final):
    bsz, t, _ = x.shape
    nb = t // ROWS
    tile = pl.BlockSpec((1, ROWS, D_MODEL), lambda b, n: (b, n, 0))
    win = pl.BlockSpec((1, WINDOW, KV_WIDTH), lambda b, n: (b, 0, 0))
    return pl.pallas_call(
        functools.partial(_attn_prompt_kernel, final=final),
        grid=(bsz, nb),
        in_specs=[tile, tile] + _attn_weight_specs(),
        out_specs=[tile, win, win],
        out_shape=[jax.ShapeDtypeStruct((bsz, t, D_MODEL), F32),
                   jax.ShapeDtypeStruct((bsz, WINDOW, KV_WIDTH), F32),
                   jax.ShapeDtypeStruct((bsz, WINDOW, KV_WIDTH), F32)],
        scratch_shapes=[pltpu.VMEM((WINDOW, KV_WIDTH), F32), pltpu.VMEM((WINDOW, KV_WIDTH), F32),
                        pltpu.VMEM((A_HEADS, WINDOW, WINDOW), F32), pltpu.VMEM((A_HEADS, WINDOW, WINDOW), F32)],
        compiler_params=_params(2),
        name="attn_prompt",
    )(x, yb, *weights)


def _attention_sample(x2d, yb, seq, weights, kc, vc, final):
    rows = x2d.shape[0]
    nseq = ROWS // seq
    tile = pl.BlockSpec((ROWS, D_MODEL), lambda i: (i, 0))
    cache = pl.BlockSpec((nseq, WINDOW, KV_WIDTH), lambda i: (i, 0, 0))
    return pl.pallas_call(
        functools.partial(_attn_sample_kernel, seq=seq, final=final),
        grid=(rows // ROWS,),
        in_specs=[tile, tile] + _attn_weight_specs() + [cache, cache],
        out_specs=[tile, cache, cache],
        out_shape=[jax.ShapeDtypeStruct((rows, D_MODEL), F32),
                   jax.ShapeDtypeStruct(kc.shape, F32), jax.ShapeDtypeStruct(vc.shape, F32)],
        scratch_shapes=[pltpu.VMEM((ROWS, A_WIDTH), F32), pltpu.VMEM((ROWS, KV_WIDTH), F32),
                        pltpu.VMEM((ROWS, KV_WIDTH), F32), pltpu.VMEM((ROWS, A_WIDTH), F32),
                        pltpu.VMEM((A_HEADS, WINDOW, WINDOW), F32), pltpu.VMEM((A_HEADS, WINDOW, WINDOW), F32)],
        compiler_params=_params(1),
        name="attn_sample",
    )(x2d, yb, *weights, kc, vc)


def _prep_scratch(names):
    shapes = {"bkt": pltpu.VMEM((N_PAIRS, LANES, 2 * ROWS), BF16), "bkh": pltpu.VMEM((N_PAIRS, 2 * ROWS, LANES), BF16)}
    return [shapes.get(n, pltpu.VMEM((ROWS, B_WIDTH), F32)) for n in names]


def _core_scratch():
    return [pltpu.VMEM((ROWS, B_WIDTH), F32), pltpu.VMEM((ROWS, 2 * B_WIDTH), F32),
            pltpu.VMEM((N_PAIRS, ROWS, 2 * ROWS), BF16), pltpu.VMEM((N_PAIRS, ROWS, 4 * ROWS), BF16)]


def _rwkv_weight_specs():
    row = _const_spec((1, B_WIDTH))
    return [_const_spec((1, D_MODEL)), _const_spec((D_MODEL, RWKV_COLS)), _const_spec((B_WIDTH, D_MODEL)),
            _const_spec((1, SHIFT_W)), row, _const_spec((LANES, 2 * B_WIDTH)), row, row, row, row, row, row]


def _rwkv_prompt(x, weights):
    bsz, t, _ = x.shape
    nb = t // ROWS
    total = bsz * nb

    def prep_tile(i):
        j = jnp.minimum(i, total - 1)
        return lax.div(j, nb), lax.rem(j, nb)

    def core_tile(i):
        j = jnp.maximum(i - 1, 0)
        return lax.div(j, nb), lax.rem(j, nb)

    return pl.pallas_call(
        functools.partial(_rwkv_prompt_kernel, tiles_per_seq=nb),
        grid=(total + 1,),
        in_specs=[pl.BlockSpec((1, ROWS, D_MODEL), lambda i: (*prep_tile(i), 0))] + _rwkv_weight_specs(),
        out_specs=[pl.BlockSpec((1, ROWS, D_MODEL), lambda i: (*core_tile(i), 0)),
                   pl.BlockSpec((1, N_PAIRS, HEAD_DIM, LANES), lambda i: (core_tile(i)[0], 0, 0, 0)),
                   pl.BlockSpec((1, 1, SHIFT_W), lambda i: (prep_tile(i)[0], 0, 0))],
        out_shape=[jax.ShapeDtypeStruct((bsz, t, D_MODEL), F32),
                   jax.ShapeDtypeStruct((bsz, N_PAIRS, HEAD_DIM, LANES), F32),
                   jax.ShapeDtypeStruct((bsz, 1, SHIFT_W), F32)],
        scratch_shapes=_prep_scratch(PREP_NAMES) + _prep_scratch(LATE_NAMES) + _core_scratch()
        + [pltpu.VMEM((N_PAIRS, LANES, LANES), F32), pltpu.VMEM((ROWS, SHIFT_W), F32),
           pltpu.VMEM((1, SHIFT_W), F32)],
        compiler_params=_params(1),
        name="rwkv_prompt",
    )(x, *weights)


def _rwkv_sample(x2d, seq, weights, s_nat, shift_rows):
    rows = x2d.shape[0]
    nseq = ROWS // seq
    tile = pl.BlockSpec((ROWS, D_MODEL), lambda i: (i, 0))
    state = pl.BlockSpec((nseq, N_PAIRS, HEAD_DIM, LANES), lambda i: (i, 0, 0, 0))
    shift = pl.BlockSpec((ROWS, SHIFT_W), lambda i: (i, 0))
    wide = pltpu.VMEM((ROWS, B_WIDTH), F32)
    return pl.pallas_call(
        functools.partial(_rwkv_sample_kernel, seq=seq),
        grid=(rows // ROWS,),
        in_specs=[tile] + _rwkv_weight_specs() + [state, shift],
        out_specs=[tile, state, shift],
        out_shape=[jax.ShapeDtypeStruct((rows, D_MODEL), F32),
                   jax.ShapeDtypeStruct(s_nat.shape, F32),
                   jax.ShapeDtypeStruct((rows, SHIFT_W), F32)],
        scratch_shapes=_prep_scratch(PREP_NAMES) + _core_scratch()
        + [wide, wide, pltpu.VMEM((N_PAIRS, ROWS, 2 * ROWS), BF16)],
        compiler_params=_params(1),
        name="rwkv_sample",
    )(x2d, *weights, s_nat, shift_rows)


def _pairs_from_heads(s):
    b = s.shape[0]
    return s.reshape(b, N_PAIRS, 2, HEAD_DIM, HEAD_DIM).transpose(0, 1, 3, 2, 4).reshape(b, N_PAIRS, HEAD_DIM, LANES)


def _heads_from_pairs(s):
    b = s.shape[0]
    return s.reshape(b, N_PAIRS, HEAD_DIM, 2, HEAD_DIM).transpose(0, 1, 3, 2, 4).reshape(b, B_HEADS, HEAD_DIM, HEAD_DIM)


def kernel(x_prompt, x_sample, cache_k_win, cache_v_win, state_wkv, state_shift, rel_bias, norm_g, w_in, attn_sinks, shift_mu, rwkv_w0, rwkv_w2, rwkv_a0, rwkv_a2, rwkv_k_k, rwkv_k_a, rwkv_r_k, lnx_g, lnx_b, w_out_a, w_out_b, w_o, final_g):
    depth = w_in.shape[0]
    bsz, t, _ = x_prompt.shape
    dbsz, dseq, _ = x_sample.shape
    hp = x_prompt
    hs = x_sample.reshape(dbsz * dseq, D_MODEL)
    fg = final_g.reshape(1, D_MODEL)
    outs = [[] for _ in range(8)]
    for l in range(depth):
        g = norm_g[l].reshape(1, D_MODEL)
        w = w_in[l]
        wa = _bf(jnp.concatenate([w[:, :ATTN_COLS], w[:, ATTN_COLS + RWKV_COLS:]], axis=1))
        wr = _bf(w[:, ATTN_COLS:ATTN_COLS + RWKV_COLS])
        woa, wob, wo = _bf(w_out_a[l]), _bf(w_out_b[l]), _bf(w_o[l])
        zeros = jnp.zeros((DECAY_LORA, B_WIDTH), F32)
        lora = _bf(jnp.concatenate([jnp.concatenate([rwkv_w2[l], zeros], axis=1),
                                    jnp.concatenate([zeros, rwkv_a2[l]], axis=1)], axis=0))
        rw = [g, wr, wob, shift_mu[l].reshape(1, SHIFT_W), rwkv_w0[l].reshape(1, B_WIDTH), lora,
              rwkv_a0[l].reshape(1, B_WIDTH), rwkv_k_k[l].reshape(1, B_WIDTH), rwkv_k_a[l].reshape(1, B_WIDTH),
              rwkv_r_k[l].reshape(1, B_WIDTH), lnx_g[l].reshape(1, B_WIDTH), lnx_b[l].reshape(1, B_WIDTH)]
        aw = [g, wa, woa, wo, fg, rel_bias, attn_sinks[l]]
        final = l == depth - 1

        yb_p, s1, t1 = _rwkv_prompt(hp, rw)
        hp, k1, v1 = _attention_prompt(hp, yb_p, aw, final)

        kc = cache_k_win[l].reshape(dbsz, WINDOW, KV_WIDTH)
        vc = cache_v_win[l].reshape(dbsz, WINDOW, KV_WIDTH)
        shift_rows = jnp.repeat(state_shift[l], dseq, axis=0)
        yb_s, s2, ps_s = _rwkv_sample(hs, dseq, rw, _pairs_from_heads(state_wkv[l]), shift_rows)
        hs, k2, v2 = _attention_sample(hs, yb_s, dseq, aw, kc, vc, final)

        outs[0].append(k1.reshape(bsz, WINDOW, A_KV_HEADS, HEAD_DIM))
        outs[1].append(v1.reshape(bsz, WINDOW, A_KV_HEADS, HEAD_DIM))
        outs[2].append(_heads_from_pairs(s1))
        outs[3].append(t1.reshape(bsz, SHIFT_W))
        outs[4].append(k2.reshape(dbsz, WINDOW, A_KV_HEADS, HEAD_DIM))
        outs[5].append(v2.reshape(dbsz, WINDOW, A_KV_HEADS, HEAD_DIM))
        outs[6].append(_heads_from_pairs(s2))
        outs[7].append(ps_s.reshape(dbsz, dseq, SHIFT_W)[:, -1])
    y_prompt = hp
    y_sample = hs.reshape(dbsz, dseq, D_MODEL)
    return (y_prompt, y_sample) + tuple(jnp.stack(o) for o in outs)
```

```python
import functools
import math

import numpy as np
import jax
import jax.numpy as jnp
from jax import lax
from jax.experimental import pallas as pl
from jax.experimental.pallas import tpu as pltpu

D_MODEL = 1024
HEAD_DIM = 64
A_HEADS = 16
A_KV_HEADS = 4
A_WIDTH = A_HEADS * HEAD_DIM
KV_WIDTH = A_KV_HEADS * HEAD_DIM
WINDOW = 128
ATTN_SCALE = HEAD_DIM ** -0.5
N_BUCKETS = 32
MAX_DISTANCE = 128
B_HEADS = 16
B_WIDTH = B_HEADS * HEAD_DIM
DECAY_LORA = 64
A_LORA = 64
SHIFT_W = 3 * B_WIDTH + DECAY_LORA + A_LORA
GN_EPS = 64e-5
NORM_EPS = 1e-6

ATTN_COLS = 2 * A_WIDTH + 2 * KV_WIDTH
RWKV_COLS = SHIFT_W + B_WIDTH
MERGE_COLS = 2 * D_MODEL

ROWS = 128
LANES = 128
N_PAIRS = B_HEADS // 2
ATTN_BLOCKS = 4
SEQ_UNROLL = 4
NEG = -1e30
VMEM_LIMIT = 56 * 1024 * 1024

F32 = jnp.float32
BF16 = jnp.bfloat16


def _bucket_ranges():
    d = np.arange(0, WINDOW + 1)
    max_exact = N_BUCKETS // 2
    df = np.maximum(d, 1).astype(np.float32)
    large = max_exact + (np.log(df / np.float32(max_exact)) / np.float32(math.log(MAX_DISTANCE / max_exact))
                         * np.float32(N_BUCKETS - max_exact)).astype(np.int32)
    large = np.minimum(large, N_BUCKETS - 1)
    bucket = np.where(d < max_exact, d, large)
    out = []
    for b in range(N_BUCKETS):
        idx = np.nonzero(bucket == b)[0]
        if idx.size:
            assert idx[-1] - idx[0] + 1 == idx.size
            out.append((b, int(idx[0]), int(idx[-1])))
    return tuple(out)


_BUCKET_RANGES = _bucket_ranges()


def _bf(x):
    return x.astype(BF16)


def _mm(a, b):
    return jnp.dot(_bf(a), _bf(b), preferred_element_type=F32)


def _mm_nt(a, b):
    return lax.dot_general(_bf(a), _bf(b), (((1,), (1,)), ((), ())), preferred_element_type=F32)


def _rms(x, g):
    ms = jnp.mean(x * x, axis=-1, keepdims=True)
    return x * lax.rsqrt(ms + NORM_EPS) * g


def _sigmoid(x):
    return 1.0 / (1.0 + jnp.exp(-x))


def _silu(x):
    return x * _sigmoid(x)


def _init_bias(relb_ref, bprev_ref, bcur_ref):
    qi = lax.broadcasted_iota(jnp.int32, (WINDOW, WINDOW), 0)
    kj = lax.broadcasted_iota(jnp.int32, (WINDOW, WINDOW), 1)
    delta = qi - kj

    def body(h, carry):
        bp = jnp.full((WINDOW, WINDOW), NEG, F32)
        bc = jnp.full((WINDOW, WINDOW), NEG, F32)
        for (b, lo, hi) in _BUCKET_RANGES:
            val = relb_ref[b, h]
            bp = jnp.where((delta >= lo - WINDOW) & (delta <= hi - WINDOW), val, bp)
            bc = jnp.where((delta >= lo) & (delta <= hi), val, bc)
        bprev_ref[h] = bp
        bcur_ref[h] = bc
        return carry

    lax.fori_loop(0, A_HEADS, body, 0)


def _attn_core(q, kp, vp, kc, vc, bprev_ref, bcur_ref, sinks_ref, tq, first):
    group = A_HEADS // A_KV_HEADS
    lo_half = lax.broadcasted_iota(jnp.int32, (1, LANES), 1) < HEAD_DIM
    hi_half = jnp.logical_not(lo_half)
    rowi = lax.broadcasted_iota(jnp.int32, (group * tq, 1), 0)
    sps, scs, sinks = [], [], []
    for kvh in range(A_KV_HEADS):
        i, c = divmod(kvh, 2)
        sl = slice(LANES * i, LANES * (i + 1))
        rows = []
        for g in range(group):
            hq = group * kvh + g
            qs = q[:, LANES * (hq // 2):LANES * (hq // 2 + 1)]
            qm = jnp.where(lo_half if hq % 2 == 0 else hi_half, qs, 0.0)
            if hq % 2 != c:
                qm = pltpu.roll(qm, HEAD_DIM, 1)
            rows.append(qm)
        lhs = _bf(jnp.concatenate(rows, axis=0))
        bp = bprev_ref[group * kvh:group * (kvh + 1), 0:tq, :].reshape(group * tq, WINDOW)
        bc = bcur_ref[group * kvh:group * (kvh + 1), 0:tq, :].reshape(group * tq, WINDOW)
        sp = _mm_nt(lhs, kp[:, sl]) + bp
        if first is not None:
            sp = jnp.where(first, NEG, sp)
        sps.append(sp)
        scs.append(_mm_nt(lhs, kc[:, sl]) + bc)
        sink = sinks_ref[group * kvh + group - 1]
        for g in range(group - 2, -1, -1):
            sink = jnp.where(rowi < (g + 1) * tq, sinks_ref[group * kvh + g], sink)
        sinks.append(sink)
    ms = [jnp.maximum(jnp.max(jnp.maximum(sp, sc), axis=-1, keepdims=True), sink)
          for sp, sc, sink in zip(sps, scs, sinks)]
    pps = [jnp.exp(sp - m) for sp, m in zip(sps, ms)]
    pcs = [jnp.exp(sc - m) for sc, m in zip(scs, ms)]
    outs = []
    for kvh in range(A_KV_HEADS):
        i, c = divmod(kvh, 2)
        sl = slice(LANES * i, LANES * (i + 1))
        own = lo_half if c == 0 else hi_half
        o = _mm(pps[kvh], jnp.where(own, vp[:, sl], 1.0)) + _mm(pcs[kvh], jnp.where(own, vc[:, sl], 1.0))
        den = pltpu.roll(o, HEAD_DIM, 1) + jnp.exp(sinks[kvh] - ms[kvh])
        outs.append(o / den)
    pieces = [None] * A_HEADS
    for kvh in range(A_KV_HEADS):
        for g in range(group):
            hq = group * kvh + g
            og = outs[kvh][g * tq:(g + 1) * tq]
            if hq % 2 != kvh % 2:
                og = pltpu.roll(og, HEAD_DIM, 1)
            pieces[hq] = og
    slabs = [jnp.where(lo_half, pieces[2 * s], pieces[2 * s + 1]) for s in range(A_HEADS // 2)]
    return jnp.concatenate(slabs, axis=1)


def _attn_project(x, g_ref, wa_ref):
    h = _rms(x, g_ref[...])
    proj = _mm(h, wa_ref[...])
    q = proj[:, :A_WIDTH] * ATTN_SCALE
    k = proj[:, A_WIDTH:A_WIDTH + KV_WIDTH]
    v = proj[:, A_WIDTH + KV_WIDTH:A_WIDTH + 2 * KV_WIDTH]
    ga = proj[:, A_WIDTH + 2 * KV_WIDTH:ATTN_COLS]
    ma = proj[:, ATTN_COLS:ATTN_COLS + D_MODEL]
    mb = proj[:, ATTN_COLS + D_MODEL:]
    return q, k, v, ga, ma, mb


def _merge_tail(x, o, ga, ma, mb, yb, woa_ref, wo_ref, fg_ref, final):
    ya = _mm(o * _silu(ga), woa_ref[...])
    merged = _sigmoid(ma) * ya + _sigmoid(mb) * yb
    out = x + _mm(merged, wo_ref[...])
    return _rms(out, fg_ref[...]) if final else out


def _attn_prompt_kernel(x_ref, yb_ref, g_ref, wa_ref, woa_ref, wo_ref, fg_ref, relb_ref, sinks_ref,
                        out_ref, kw_ref, vw_ref, kprev, vprev, bprev, bcur, *, final):
    b = pl.program_id(0)
    n = pl.program_id(1)

    @pl.when((b == 0) & (n == 0))
    def _():
        _init_bias(relb_ref, bprev, bcur)

    @pl.when(n == 0)
    def _():
        kprev[...] = jnp.zeros_like(kprev)
        vprev[...] = jnp.zeros_like(vprev)

    x = x_ref[0]
    q, k, v, ga, ma, mb = _attn_project(x, g_ref, wa_ref)
    kp, vp = kprev[...], vprev[...]
    os = []
    for j in range(ATTN_BLOCKS):
        rows = slice(WINDOW * j, WINDOW * (j + 1))
        os.append(_attn_core(q[rows], kp, vp, k[rows], v[rows], bprev, bcur, sinks_ref, WINDOW,
                             (n == 0) if j == 0 else None))
        kp, vp = k[rows], v[rows]
    kprev[...] = kp
    vprev[...] = vp
    kw_ref[0] = kp
    vw_ref[0] = vp
    o = jnp.concatenate(os, axis=0)
    out_ref[0] = _merge_tail(x, o, ga, ma, mb, yb_ref[0], woa_ref, wo_ref, fg_ref, final)


def _attn_sample_kernel(x_ref, yb_ref, g_ref, wa_ref, woa_ref, wo_ref, fg_ref, relb_ref, sinks_ref, kc_ref, vc_ref,
                        out_ref, ko_ref, vo_ref, qbuf, kbuf, vbuf, obuf, bprev, bcur, *, seq, final):
    @pl.when(pl.program_id(0) == 0)
    def _():
        _init_bias(relb_ref, bprev, bcur)

    x = x_ref[...]
    q, k, v, ga, ma, mb = _attn_project(x, g_ref, wa_ref)
    qbuf[...] = q
    kbuf[...] = k
    vbuf[...] = v
    pad = jnp.zeros((WINDOW - seq, KV_WIDTH), F32)

    def body(j, carry):
        rows = pl.ds(pl.multiple_of(j * seq, seq), seq)
        kn = kbuf[rows, :]
        vn = vbuf[rows, :]
        kcj = kc_ref[j]
        vcj = vc_ref[j]
        o = _attn_core(qbuf[rows, :], kcj, vcj, jnp.concatenate([kn, pad], axis=0),
                       jnp.concatenate([vn, pad], axis=0), bprev, bcur, sinks_ref, seq, None)
        obuf[rows, :] = o
        ko_ref[j, pl.ds(0, WINDOW - seq), :] = kcj[seq:, :]
        ko_ref[j, pl.ds(WINDOW - seq, seq), :] = kn
        vo_ref[j, pl.ds(0, WINDOW - seq), :] = vcj[seq:, :]
        vo_ref[j, pl.ds(WINDOW - seq, seq), :] = vn
        return carry

    lax.fori_loop(0, ROWS // seq, body, 0, unroll=SEQ_UNROLL)
    out_ref[...] = _merge_tail(x, obuf[...], ga, ma, mb, yb_ref[...], woa_ref, wo_ref, fg_ref, final)


def _head_sums(x, ones_blocks, terms=2):
    outs = []
    width = ones_blocks.shape[0]
    for i in range(x.shape[1] // width):
        xs = x[:, width * i:width * (i + 1)]
        hi = _bf(xs)
        acc = jnp.dot(hi, ones_blocks, preferred_element_type=F32)
        if terms == 2:
            acc = acc + jnp.dot(_bf(xs - hi.astype(F32)), ones_blocks, preferred_element_type=F32)
        outs.append(acc)
    return jnp.concatenate(outs, axis=1)


def _seg_cumsum(x, seq):
    pos = lax.broadcasted_iota(jnp.int32, (x.shape[0], 1), 0) & (seq - 1)
    s = 1
    while s < seq:
        if s % 8 == 0 and seq == x.shape[0]:
            x = jnp.concatenate([x[:s], x[s:] + x[:-s]], axis=0)
        else:
            x = x + jnp.where(pos >= s, pltpu.roll(x, s, 0), 0.0)
        s *= 2
    return x


def _neumann_levels(ps, seq, out):
    n = ps[0].shape[0]
    ri = lax.broadcasted_iota(jnp.int32, (n, n), 0)
    ci = lax.broadcasted_iota(jnp.int32, (n, n), 1)
    eye = jnp.where(ri == ci, 1.0, 0.0)
    xs = [eye + p for p in ps]
    if seq > 2:
        pws = [_mm(p, p) for p in ps]
        yield
        span = 2
        while span < seq:
            last = 2 * span >= seq
            skip = span if span >= 16 else 0
            nxt_x, nxt_p = [], []
            for pw, x in zip(pws, xs):
                rhs = x if last else jnp.concatenate([pw, x], axis=1)
                res = _mm(pw[skip:], rhs)
                dx = res if last else res[:, n:]
                if skip:
                    nxt_x.append(jnp.concatenate([x[:skip], x[skip:] + dx], axis=0))
                else:
                    nxt_x.append(x + dx)
                if not last:
                    pn = res[:, :n]
                    nxt_p.append(jnp.concatenate([jnp.zeros((skip, n), F32), pn], axis=0) if skip else pn)
            xs, pws = nxt_x, nxt_p
            span *= 2
            yield
    out[:] = xs


PREP_NAMES = ("at", "rt", "bkt", "v", "gt", "bonus", "sgb", "bkh")
RWKV_WEIGHT_NAMES = ("g", "wr", "wob", "mu", "w0", "lora", "a0", "kk", "ka", "rk", "lng", "lnb")
EARLY_NAMES = ("at", "rt", "bkt")
LATE_NAMES = tuple(n for n in PREP_NAMES if n not in EARLY_NAMES)
PREP_SLAB = 2 * LANES


def _ones_blocks():
    ri = lax.broadcasted_iota(jnp.int32, (2 * LANES, 2 * LANES), 0)
    ci = lax.broadcasted_iota(jnp.int32, (2 * LANES, 2 * LANES), 1)
    return jnp.where((ri >> 6) == (ci >> 6), 1.0, 0.0).astype(BF16)


def _rwkv_prep(x, first_ref, w, seq, out, ps_ref, lora_ref):
    nseq = ROWS // seq
    h = _rms(x, w["g"][...])
    proj = _mm(h, w["wr"][...])
    ps_ref[...] = proj[:, :SHIFT_W]
    out["sgb"][...] = _silu(proj[:, SHIFT_W:])
    pos = lax.broadcasted_iota(jnp.int32, (ROWS, 1), 0) & (seq - 1)

    def shifted(cols):
        cur = ps_ref[:, cols]
        prev = jnp.where(pos == 0, first_ref[:, cols], pltpu.roll(cur, 1, 0))
        return cur + (prev - cur) * w["mu"][:, cols]

    zl = shifted(slice(3 * B_WIDTH, SHIFT_W))
    lo_half = lax.broadcasted_iota(jnp.int32, (1, LANES), 1) < HEAD_DIM
    lora_ref[...] = _mm(jnp.where(lo_half, jnp.tanh(zl), zl), w["lora"][...])
    yield
    ones_blocks = _ones_blocks()
    for d in range(B_WIDTH // PREP_SLAB):
        cols = slice(PREP_SLAB * d, PREP_SLAB * (d + 1))

        def wcols(name):
            return w[name][:, cols]

        r = shifted(cols)
        k = shifted(slice(B_WIDTH + cols.start, B_WIDTH + cols.stop))
        v = shifted(slice(2 * B_WIDTH + cols.start, 2 * B_WIDTH + cols.stop))
        lw = -math.exp(-0.5) * _sigmoid(wcols("w0") + lora_ref[:, cols])
        asig = _sigmoid(wcols("a0") + lora_ref[:, slice(B_WIDTH + cols.start, B_WIDTH + cols.stop)])
        kkr = k * wcols("kk")
        kk = kkr * lax.rsqrt(jnp.maximum(_head_sums(kkr * kkr, ones_blocks), 1e-24))
        kp = k * (1.0 + (asig - 1.0) * wcols("ka"))
        a_ = -kk
        b_ = kk * asig
        cum = _seg_cumsum(lw, seq)
        if nseq == 1:
            tot = jnp.broadcast_to(cum[ROWS - 1:ROWS, :], cum.shape)
        else:
            c3 = cum.reshape(nseq, seq, PREP_SLAB)
            tot = jnp.broadcast_to(c3[:, seq - 1:seq, :], c3.shape).reshape(cum.shape)
        g_inv = jnp.exp(-cum)
        g_last = jnp.exp(tot - cum)
        out["at"][:, cols] = a_ * jnp.exp(cum - lw)
        out["rt"][:, cols] = r * jnp.exp(cum)
        bt = b_ * g_inv
        kt = kp * g_inv
        bh = b_ * g_last
        kh = kp * g_last
        for q in range(PREP_SLAB // LANES):
            ql = slice(LANES * q, LANES * (q + 1))
            pair = PREP_SLAB // LANES * d + q
            out["bkt"][pair] = _bf(jnp.concatenate([bt[:, ql], kt[:, ql]], axis=0).T)
            out["bkh"][pair] = _bf(jnp.concatenate([bh[:, ql], kh[:, ql]], axis=0))
        out["gt"][:, cols] = jnp.exp(tot)
        out["v"][:, cols] = v
        out["bonus"][:, cols] = _head_sums(r * kp * wcols("rk"), ones_blocks) * v
        yield


def _rwkv_core(buf, w, seq, av_ref, t_ref, arbk_ref, res, *, s_ref=None, si_ref=None, so_ref=None,
               x0_ref=None, y0_ref=None, uvt_ref=None):
    nseq = ROWS // seq
    prompt = s_ref is not None
    lo_half = lax.broadcasted_iota(jnp.int32, (1, LANES), 1) < HEAD_DIM
    hi_half = jnp.logical_not(lo_half)
    rr = lax.broadcasted_iota(jnp.int32, (ROWS, ROWS), 0)
    cc = lax.broadcasted_iota(jnp.int32, (ROWS, ROWS), 1)
    shift = int(math.log2(seq))
    same = (rr >> shift) == (cc >> shift)
    strict = same & (rr > cc)
    incl = same & (rr >= cc)
    bd = (rr < HEAD_DIM) == (cc < HEAD_DIM)
    pair_slices = [slice(LANES * p, LANES * (p + 1)) for p in range(N_PAIRS)]
    at_ref, rt_ref, bkt_ref, v_ref, gt_ref, bkh_ref = (buf[n] for n in ("at", "rt", "bkt", "v", "gt", "bkh"))

    def halves(t):
        return jnp.concatenate([jnp.where(lo_half, t, 0.0), jnp.where(hi_half, t, 0.0)], axis=0)

    def blockdiag(s_nat):
        return jnp.concatenate([jnp.where(lo_half, s_nat, 0.0), jnp.where(lo_half, 0.0, s_nat)], axis=0)

    if prompt:
        s0s = [s_ref[p] for p in range(N_PAIRS)]
        s0ts = [_bf(s0.T) for s0 in s0s]
        x0s = [_mm(at_ref[:, sl], s0t) for sl, s0t in zip(pair_slices, s0ts)]
        y0s = [_mm(rt_ref[:, sl], s0t) for sl, s0t in zip(pair_slices, s0ts)]
    else:
        def read_body(j, carry):
            rows = pl.ds(pl.multiple_of(j * seq, seq), seq)
            for p, sl in enumerate(pair_slices):
                s0 = blockdiag(si_ref[j, p])
                arj = jnp.concatenate([at_ref[rows, sl], rt_ref[rows, sl]], axis=0)
                xy = _mm_nt(arj, s0)
                x0_ref[rows, sl] = xy[:seq]
                y0_ref[rows, sl] = xy[seq:]
            return carry
        lax.fori_loop(0, nseq, read_body, 0, unroll=SEQ_UNROLL)
        x0s = [x0_ref[:, sl] for sl in pair_slices]
        y0s = [y0_ref[:, sl] for sl in pair_slices]
    yield

    a_abs, a_aks = [], []
    for p, sl in enumerate(pair_slices):
        rbk = []
        for hh in range(2):
            half = lo_half if hh == 0 else hi_half
            ga = _mm(jnp.where(half, at_ref[:, sl], 0.0), bkt_ref[p])
            gr = _mm(jnp.where(half, rt_ref[:, sl], 0.0), bkt_ref[p])
            a_abs.append(jnp.where(strict, ga[:, :ROWS], 0.0))
            a_aks.append(jnp.where(strict, ga[:, ROWS:], 0.0))
            rbk.append(jnp.where(incl, gr[:, :ROWS], 0.0))
            rbk.append(jnp.where(incl, gr[:, ROWS:], 0.0))
        arbk_ref[p] = _bf(jnp.concatenate(rbk, axis=1))
        av_ref[:, sl] = _mm(jnp.concatenate(a_aks[2 * p:2 * p + 2], axis=1), halves(v_ref[:, sl]))
        if p % 4 == 3:
            yield
    ts = []
    yield from _neumann_levels(a_abs, seq, ts)
    for p in range(N_PAIRS):
        t_ref[p] = _bf(jnp.concatenate(ts[2 * p:2 * p + 2], axis=1))

    vss = [v_ref[:, sl] for sl in pair_slices]
    us = [jnp.dot(t_ref[p], _bf(halves(x0s[p] + av_ref[:, sl])), preferred_element_type=F32)
          for p, sl in enumerate(pair_slices)]
    yield
    ys = []
    for p in range(N_PAIRS):
        u, vs = us[p], vss[p]
        rhs = jnp.concatenate([jnp.where(lo_half, u, 0.0), jnp.where(lo_half, vs, 0.0),
                               jnp.where(hi_half, u, 0.0), jnp.where(hi_half, vs, 0.0)], axis=0)
        ys.append(y0s[p] + jnp.dot(arbk_ref[p], _bf(rhs), preferred_element_type=F32))
    yield

    uvts = [jnp.concatenate([us[p], vss[p]], axis=0).T for p in range(N_PAIRS)]
    if prompt:
        s1s = [s0s[p] * gt_ref[0:1, sl] + jnp.where(bd, _mm(uvts[p], bkh_ref[p]), 0.0)
               for p, sl in enumerate(pair_slices)]
        for p in range(N_PAIRS):
            s_ref[p] = s1s[p]
        res["s1"] = s1s
    else:
        colseq = (lax.broadcasted_iota(jnp.int32, (1, 2 * ROWS), 1) & (ROWS - 1)) >> shift
        for p in range(N_PAIRS):
            uvt_ref[p] = _bf(uvts[p])

        def upd_body(j, carry):
            row = pl.ds(pl.multiple_of(j * seq, seq), 1)
            for p, sl in enumerate(pair_slices):
                uvt = jnp.where(colseq == j, uvt_ref[p], jnp.zeros((), BF16))
                upd = jnp.dot(uvt, bkh_ref[p], preferred_element_type=F32)
                so_ref[j, p] = (si_ref[j, p] * gt_ref[row, sl]
                                + jnp.where(lo_half, upd[:HEAD_DIM], upd[HEAD_DIM:]))
            return carry
        lax.fori_loop(0, nseq, upd_body, 0, unroll=SEQ_UNROLL)
    yield

    ones_blocks = _ones_blocks()
    y = jnp.concatenate(ys, axis=1)
    mean = _head_sums(y, ones_blocks, terms=1) * (1.0 / HEAD_DIM)
    dlt = y - mean
    var = _head_sums(dlt * dlt, ones_blocks, terms=1) * (1.0 / HEAD_DIM)
    yn = dlt * lax.rsqrt(var + GN_EPS) * w["lng"][...] + w["lnb"][...]
    yo = (yn + buf["bonus"][...]) * buf["sgb"][...]
    res["yb"] = _mm(yo, w["wob"][...])


N_RWKV_W = len(RWKV_WEIGHT_NAMES)
N_PREP = len(PREP_NAMES)
PROMPT_ORDER = "CCCCPCCPCCPCCPCPCC"


def _rwkv_prompt_kernel(*refs, tiles_per_seq):
    x_ref = refs[0]
    w = dict(zip(RWKV_WEIGHT_NAMES, refs[1:1 + N_RWKV_W]))
    yb_ref, so_ref, sh_ref = refs[1 + N_RWKV_W:4 + N_RWKV_W]
    scratch = refs[4 + N_RWKV_W:]
    cur = dict(zip(PREP_NAMES, scratch[:N_PREP]))
    nxt = dict(cur)
    nxt.update(zip(LATE_NAMES, scratch[N_PREP:N_PREP + len(LATE_NAMES)]))
    av_ref, lora_ref, t_ref, arbk_ref, s_ref, ps_ref, carry_ref = scratch[N_PREP + len(LATE_NAMES):]
    i = pl.program_id(0)
    n_prep = lax.rem(i, tiles_per_seq)

    @pl.when(i == 0)
    def _():
        for name in PREP_NAMES:
            cur[name][...] = jnp.zeros_like(cur[name])

    @pl.when(n_prep == 0)
    def _():
        carry_ref[...] = jnp.zeros_like(carry_ref)

    @pl.when((n_prep == 1) | (i == 0))
    def _():
        s_ref[...] = jnp.zeros_like(s_ref)

    res = {}
    prep = _rwkv_prep(x_ref[0], carry_ref, w, ROWS, nxt, ps_ref, lora_ref)
    core = _rwkv_core(cur, w, ROWS, av_ref, t_ref, arbk_ref, res, s_ref=s_ref)
    for kind in PROMPT_ORDER:
        next(prep if kind == "P" else core, None)
    for _ in prep:
        pass
    for _ in core:
        pass
    carry_ref[...] = ps_ref[ROWS - 1:ROWS, :]
    sh_ref[0] = ps_ref[ROWS - 1:ROWS, :]
    yb_ref[0] = res["yb"]

    @pl.when((n_prep == 0) & (i > 0))
    def _():
        lo_half = lax.broadcasted_iota(jnp.int32, (1, LANES), 1) < HEAD_DIM
        for p in range(N_PAIRS):
            so_ref[0, p] = jnp.where(lo_half, res["s1"][p][:HEAD_DIM], res["s1"][p][HEAD_DIM:])

    for name in LATE_NAMES:
        if name == "gt":
            cur[name][0:1, :] = nxt[name][0:1, :]
        else:
            cur[name][...] = nxt[name][...]


def _rwkv_sample_kernel(*refs, seq):
    x_ref = refs[0]
    w = dict(zip(RWKV_WEIGHT_NAMES, refs[1:1 + N_RWKV_W]))
    si_ref, shin_ref, yb_ref, so_ref, ps_ref = refs[1 + N_RWKV_W:6 + N_RWKV_W]
    scratch = refs[6 + N_RWKV_W:]
    buf = dict(zip(PREP_NAMES, scratch[:N_PREP]))
    av_ref, lora_ref, t_ref, arbk_ref, x0_ref, y0_ref, uvt_ref = scratch[N_PREP:]
    res = {}
    for _ in _rwkv_prep(x_ref[...], shin_ref, w, seq, buf, ps_ref, lora_ref):
        pass
    for _ in _rwkv_core(buf, w, seq, av_ref, t_ref, arbk_ref, res, si_ref=si_ref, so_ref=so_ref,
                        x0_ref=x0_ref, y0_ref=y0_ref, uvt_ref=uvt_ref):
        pass
    yb_ref[...] = res["yb"]


def _const_spec(shape):
    nd = len(shape)
    return pl.BlockSpec(shape, lambda *_: (0,) * nd, pipeline_mode=pl.Buffered(1))


def _smem_spec():
    return pl.BlockSpec(memory_space=pltpu.SMEM)


def _params(n_axes):
    return pltpu.CompilerParams(dimension_semantics=("arbitrary",) * n_axes, vmem_limit_bytes=VMEM_LIMIT)


def _attn_weight_specs():
    return [_const_spec((1, D_MODEL)), _const_spec((D_MODEL, ATTN_COLS + MERGE_COLS)),
            _const_spec((A_WIDTH, D_MODEL)), _const_spec((D_MODEL, D_MODEL)), _const_spec((1, D_MODEL)),
            _smem_spec(), _smem_spec()]


def _attention_prompt(x, yb, weights, final):
    bsz, t, _ = x.shape
    rows = ATTN_BLOCKS * WINDOW
    nb = t // rows
    tile = pl.BlockSpec((1, rows, D_MODEL), lambda b, n: (b, n, 0))
    win = pl.BlockSpec((1, WINDOW, KV_WIDTH), lambda b, n: (b, 0, 0))
    return pl.pallas_call(
        functools.partial(_attn_prompt_kernel, final=final),
        grid=(bsz, nb),
        in_specs=[tile, tile] + _attn_weight_specs(),
        out_specs=[tile, win, win],
        out_shape=[jax.ShapeDtypeStruct((bsz, t, D_MODEL), F32),
                   jax.ShapeDtypeStruct((bsz, WINDOW, KV_WIDTH), F32),
                   jax.ShapeDtypeStruct((bsz, WINDOW, KV_WIDTH), F32)],
        scratch_shapes=[pltpu.VMEM((WINDOW, KV_WIDTH), F32), pltpu.VMEM((WINDOW, KV_WIDTH), F32),
                        pltpu.VMEM((A_HEADS, WINDOW, WINDOW), F32), pltpu.VMEM((A_HEADS, WINDOW, WINDOW), F32)],
        compiler_params=_params(2),
        name="attn_prompt",
    )(x, yb, *weights)


def _attention_sample(x2d, yb, seq, weights, kc, vc, final):
    rows = x2d.shape[0]
    nseq = ROWS // seq
    tile = pl.BlockSpec((ROWS, D_MODEL), lambda i: (i, 0))
    cache = pl.BlockSpec((nseq, WINDOW, KV_WIDTH), lambda i: (i, 0, 0))
    return pl.pallas_call(
        functools.partial(_attn_sample_kernel, seq=seq, final=final),
        grid=(rows // ROWS,),
        in_specs=[tile, tile] + _attn_weight_specs() + [cache, cache],
        out_specs=[tile, cache, cache],
        out_shape=[jax.ShapeDtypeStruct((rows, D_MODEL), F32),
                   jax.ShapeDtypeStruct(kc.shape, F32), jax.ShapeDtypeStruct(vc.shape, F32)],
        scratch_shapes=[pltpu.VMEM((ROWS, A_WIDTH), F32), pltpu.VMEM((ROWS, KV_WIDTH), F32),
                        pltpu.VMEM((ROWS, KV_WIDTH), F32), pltpu.VMEM((ROWS, A_WIDTH), F32),
                        pltpu.VMEM((A_HEADS, WINDOW, WINDOW), F32), pltpu.VMEM((A_HEADS, WINDOW, WINDOW), F32)],
        compiler_params=_params(1),
        name="attn_sample",
    )(x2d, yb, *weights, kc, vc)


def _prep_scratch(names):
    shapes = {"bkt": pltpu.VMEM((N_PAIRS, LANES, 2 * ROWS), BF16), "bkh": pltpu.VMEM((N_PAIRS, 2 * ROWS, LANES), BF16)}
    return [shapes.get(n, pltpu.VMEM((ROWS, B_WIDTH), F32)) for n in names]


def _core_scratch():
    return [pltpu.VMEM((ROWS, B_WIDTH), F32), pltpu.VMEM((ROWS, 2 * B_WIDTH), F32),
            pltpu.VMEM((N_PAIRS, ROWS, 2 * ROWS), BF16), pltpu.VMEM((N_PAIRS, ROWS, 4 * ROWS), BF16)]


def _rwkv_weight_specs():
    row = _const_spec((1, B_WIDTH))
    return [_const_spec((1, D_MODEL)), _const_spec((D_MODEL, RWKV_COLS)), _const_spec((B_WIDTH, D_MODEL)),
            _const_spec((1, SHIFT_W)), row, _const_spec((LANES, 2 * B_WIDTH)), row, row, row, row, row, row]


def _rwkv_prompt(x, weights):
    bsz, t, _ = x.shape
    nb = t // ROWS
    total = bsz * nb

    def prep_tile(i):
        j = jnp.minimum(i, total - 1)
        return lax.div(j, nb), lax.rem(j, nb)

    def core_tile(i):
        j = jnp.maximum(i - 1, 0)
        return lax.div(j, nb), lax.rem(j, nb)

    return pl.pallas_call(
        functools.partial(_rwkv_prompt_kernel, tiles_per_seq=nb),
        grid=(total + 1,),
        in_specs=[pl.BlockSpec((1, ROWS, D_MODEL), lambda i: (*prep_tile(i), 0))] + _rwkv_weight_specs(),
        out_specs=[pl.BlockSpec((1, ROWS, D_MODEL), lambda i: (*core_tile(i), 0)),
                   pl.BlockSpec((1, N_PAIRS, HEAD_DIM, LANES), lambda i: (core_tile(i)[0], 0, 0, 0)),
                   pl.BlockSpec((1, 1, SHIFT_W), lambda i: (prep_tile(i)[0], 0, 0))],
        out_shape=[jax.ShapeDtypeStruct((bsz, t, D_MODEL), F32),
                   jax.ShapeDtypeStruct((bsz, N_PAIRS, HEAD_DIM, LANES), F32),
                   jax.ShapeDtypeStruct((bsz, 1, SHIFT_W), F32)],
        scratch_shapes=_prep_scratch(PREP_NAMES) + _prep_scratch(LATE_NAMES) + _core_scratch()
        + [pltpu.VMEM((N_PAIRS, LANES, LANES), F32), pltpu.VMEM((ROWS, SHIFT_W), F32),
           pltpu.VMEM((1, SHIFT_W), F32)],
        compiler_params=_params(1),
        name="rwkv_prompt",
    )(x, *weights)


def _rwkv_sample(x2d, seq, weights, s_nat, shift_rows):
    rows = x2d.shape[0]
    nseq = ROWS // seq
    tile = pl.BlockSpec((ROWS, D_MODEL), lambda i: (i, 0))
    state = pl.BlockSpec((nseq, N_PAIRS, HEAD_DIM, LANES), lambda i: (i, 0, 0, 0))
    shift = pl.BlockSpec((ROWS, SHIFT_W), lambda i: (i, 0))
    wide = pltpu.VMEM((ROWS, B_WIDTH), F32)
    return pl.pallas_call(
        functools.partial(_rwkv_sample_kernel, seq=seq),
        grid=(rows // ROWS,),
        in_specs=[tile] + _rwkv_weight_specs() + [state, shift],
        out_specs=[tile, state, shift],
        out_shape=[jax.ShapeDtypeStruct((rows, D_MODEL), F32),
                   jax.ShapeDtypeStruct(s_nat.shape, F32),
                   jax.ShapeDtypeStruct((rows, SHIFT_W), F32)],
        scratch_shapes=_prep_scratch(PREP_NAMES) + _core_scratch()
        + [wide, wide, pltpu.VMEM((N_PAIRS, ROWS, 2 * ROWS), BF16)],
        compiler_params=_params(1),
        name="rwkv_sample",
    )(x2d, *weights, s_nat, shift_rows)


def _pairs_from_heads(s):
    b = s.shape[0]
    return s.reshape(b, N_PAIRS, 2, HEAD_DIM, HEAD_DIM).transpose(0, 1, 3, 2, 4).reshape(b, N_PAIRS, HEAD_DIM, LANES)


def _heads_from_pairs(s):
    b = s.shape[0]
    return s.reshape(b, N_PAIRS, HEAD_DIM, 2, HEAD_DIM).transpose(0, 1, 3, 2, 4).reshape(b, B_HEADS, HEAD_DIM, HEAD_DIM)


def kernel(x_prompt, x_sample, cache_k_win, cache_v_win, state_wkv, state_shift, rel_bias, norm_g, w_in, attn_sinks, shift_mu, rwkv_w0, rwkv_w2, rwkv_a0, rwkv_a2, rwkv_k_k, rwkv_k_a, rwkv_r_k, lnx_g, lnx_b, w_out_a, w_out_b, w_o, final_g):
    depth = w_in.shape[0]
    bsz, t, _ = x_prompt.shape
    dbsz, dseq, _ = x_sample.shape
    hp = x_prompt
    hs = x_sample.reshape(dbsz * dseq, D_MODEL)
    fg = final_g.reshape(1, D_MODEL)
    outs = [[] for _ in range(8)]
    for l in range(depth):
        g = norm_g[l].reshape(1, D_MODEL)
        w = w_in[l]
        wa = _bf(jnp.concatenate([w[:, :ATTN_COLS], w[:, ATTN_COLS + RWKV_COLS:]], axis=1))
        wr = _bf(w[:, ATTN_COLS:ATTN_COLS + RWKV_COLS])
        woa, wob, wo = _bf(w_out_a[l]), _bf(w_out_b[l]), _bf(w_o[l])
        zeros = jnp.zeros((DECAY_LORA, B_WIDTH), F32)
        lora = _bf(jnp.concatenate([jnp.concatenate([rwkv_w2[l], zeros], axis=1),
                                    jnp.concatenate([zeros, rwkv_a2[l]], axis=1)], axis=0))
        rw = [g, wr, wob, shift_mu[l].reshape(1, SHIFT_W), rwkv_w0[l].reshape(1, B_WIDTH), lora,
              rwkv_a0[l].reshape(1, B_WIDTH), rwkv_k_k[l].reshape(1, B_WIDTH), rwkv_k_a[l].reshape(1, B_WIDTH),
              rwkv_r_k[l].reshape(1, B_WIDTH), lnx_g[l].reshape(1, B_WIDTH), lnx_b[l].reshape(1, B_WIDTH)]
        aw = [g, wa, woa, wo, fg, rel_bias, attn_sinks[l]]
        final = l == depth - 1

        yb_p, s1, t1 = _rwkv_prompt(hp, rw)
        hp, k1, v1 = _attention_prompt(hp, yb_p, aw, final)

        kc = cache_k_win[l].reshape(dbsz, WINDOW, KV_WIDTH)
        vc = cache_v_win[l].reshape(dbsz, WINDOW, KV_WIDTH)
        shift_rows = jnp.repeat(state_shift[l], dseq, axis=0)
        yb_s, s2, ps_s = _rwkv_sample(hs, dseq, rw, _pairs_from_heads(state_wkv[l]), shift_rows)
        hs, k2, v2 = _attention_sample(hs, yb_s, dseq, aw, kc, vc, final)

        outs[0].append(k1.reshape(bsz, WINDOW, A_KV_HEADS, HEAD_DIM))
        outs[1].append(v1.reshape(bsz, WINDOW, A_KV_HEADS, HEAD_DIM))
        outs[2].append(_heads_from_pairs(s1))
        outs[3].append(t1.reshape(bsz, SHIFT_W))
        outs[4].append(k2.reshape(dbsz, WINDOW, A_KV_HEADS, HEAD_DIM))
        outs[5].append(v2.reshape(dbsz, WINDOW, A_KV_HEADS, HEAD_DIM))
        outs[6].append(_heads_from_pairs(s2))
        outs[7].append(ps_s.reshape(dbsz, dseq, SHIFT_W)[:, -1])
    y_prompt = hp
    y_sample = hs.reshape(dbsz, dseq, D_MODEL)
    return (y_prompt, y_sample) + tuple(jnp.stack(o) for o in outs)
```

```python
import functools
import math

import numpy as np
import jax
import jax.numpy as jnp
from jax import lax
from jax.experimental import pallas as pl
from jax.experimental.pallas import tpu as pltpu

D_MODEL = 1024
HEAD_DIM = 64
A_HEADS = 16
A_KV_HEADS = 4
A_WIDTH = A_HEADS * HEAD_DIM
KV_WIDTH = A_KV_HEADS * HEAD_DIM
WINDOW = 128
ATTN_SCALE = HEAD_DIM ** -0.5
N_BUCKETS = 32
MAX_DISTANCE = 128
B_HEADS = 16
B_WIDTH = B_HEADS * HEAD_DIM
DECAY_LORA = 64
A_LORA = 64
SHIFT_W = 3 * B_WIDTH + DECAY_LORA + A_LORA
GN_EPS = 64e-5
NORM_EPS = 1e-6

ATTN_COLS = 2 * A_WIDTH + 2 * KV_WIDTH
RWKV_COLS = SHIFT_W + B_WIDTH
MERGE_COLS = 2 * D_MODEL

ROWS = 128
LANES = 128
N_PAIRS = B_HEADS // 2
ATTN_BLOCKS = 4
SEQ_UNROLL = 4
NEG = -1e30
VMEM_LIMIT = 56 * 1024 * 1024

F32 = jnp.float32
BF16 = jnp.bfloat16


def _bucket_ranges():
    d = np.arange(0, WINDOW + 1)
    max_exact = N_BUCKETS // 2
    df = np.maximum(d, 1).astype(np.float32)
    large = max_exact + (np.log(df / np.float32(max_exact)) / np.float32(math.log(MAX_DISTANCE / max_exact))
                         * np.float32(N_BUCKETS - max_exact)).astype(np.int32)
    large = np.minimum(large, N_BUCKETS - 1)
    bucket = np.where(d < max_exact, d, large)
    out = []
    for b in range(N_BUCKETS):
        idx = np.nonzero(bucket == b)[0]
        if idx.size:
            assert idx[-1] - idx[0] + 1 == idx.size
            out.append((b, int(idx[0]), int(idx[-1])))
    return tuple(out)


_BUCKET_RANGES = _bucket_ranges()


def _bf(x):
    return x.astype(BF16)


def _mm(a, b):
    return jnp.dot(_bf(a), _bf(b), preferred_element_type=F32)


def _mm_nt(a, b):
    return lax.dot_general(_bf(a), _bf(b), (((1,), (1,)), ((), ())), preferred_element_type=F32)


def _rms(x, g):
    ms = jnp.mean(x * x, axis=-1, keepdims=True)
    return x * lax.rsqrt(ms + NORM_EPS) * g


def _sigmoid(x):
    return 1.0 / (1.0 + jnp.exp(-x))


def _silu(x):
    return x * _sigmoid(x)


def _init_bias(relb_ref, bprev_ref, bcur_ref):
    qi = lax.broadcasted_iota(jnp.int32, (WINDOW, WINDOW), 0)
    kj = lax.broadcasted_iota(jnp.int32, (WINDOW, WINDOW), 1)
    delta = qi - kj

    def body(h, carry):
        bp = jnp.full((WINDOW, WINDOW), NEG, F32)
        bc = jnp.full((WINDOW, WINDOW), NEG, F32)
        for (b, lo, hi) in _BUCKET_RANGES:
            val = relb_ref[b, h]
            bp = jnp.where((delta >= lo - WINDOW) & (delta <= hi - WINDOW), val, bp)
            bc = jnp.where((delta >= lo) & (delta <= hi), val, bc)
        bprev_ref[h] = bp
        bcur_ref[h] = bc
        return carry

    lax.fori_loop(0, A_HEADS, body, 0)


def _attn_core(q, kp, vp, kc, vc, bprev_ref, bcur_ref, sinks_ref, tq, first):
    group = A_HEADS // A_KV_HEADS
    lo_half = lax.broadcasted_iota(jnp.int32, (1, LANES), 1) < HEAD_DIM
    hi_half = jnp.logical_not(lo_half)
    rowi = lax.broadcasted_iota(jnp.int32, (group * tq, 1), 0)
    sps, scs, sinks = [], [], []
    for kvh in range(A_KV_HEADS):
        i, c = divmod(kvh, 2)
        sl = slice(LANES * i, LANES * (i + 1))
        rows = []
        for g in range(group):
            hq = group * kvh + g
            qs = q[:, LANES * (hq // 2):LANES * (hq // 2 + 1)]
            qm = jnp.where(lo_half if hq % 2 == 0 else hi_half, qs, 0.0)
            if hq % 2 != c:
                qm = pltpu.roll(qm, HEAD_DIM, 1)
            rows.append(qm)
        lhs = _bf(jnp.concatenate(rows, axis=0))
        bp = bprev_ref[group * kvh:group * (kvh + 1), 0:tq, :].reshape(group * tq, WINDOW)
        bc = bcur_ref[group * kvh:group * (kvh + 1), 0:tq, :].reshape(group * tq, WINDOW)
        sp = _mm_nt(lhs, kp[:, sl]) + bp
        if first is not None:
            sp = jnp.where(first, NEG, sp)
        sps.append(sp)
        scs.append(_mm_nt(lhs, kc[:, sl]) + bc)
        sink = sinks_ref[group * kvh + group - 1]
        for g in range(group - 2, -1, -1):
            sink = jnp.where(rowi < (g + 1) * tq, sinks_ref[group * kvh + g], sink)
        sinks.append(sink)
    ms = [jnp.maximum(jnp.max(jnp.maximum(sp, sc), axis=-1, keepdims=True), sink)
          for sp, sc, sink in zip(sps, scs, sinks)]
    pps = [jnp.exp(sp - m) for sp, m in zip(sps, ms)]
    pcs = [jnp.exp(sc - m) for sc, m in zip(scs, ms)]
    outs = []
    for kvh in range(A_KV_HEADS):
        i, c = divmod(kvh, 2)
        sl = slice(LANES * i, LANES * (i + 1))
        own = lo_half if c == 0 else hi_half
        o = _mm(pps[kvh], jnp.where(own, vp[:, sl], 1.0)) + _mm(pcs[kvh], jnp.where(own, vc[:, sl], 1.0))
        den = pltpu.roll(o, HEAD_DIM, 1) + jnp.exp(sinks[kvh] - ms[kvh])
        outs.append(o / den)
    pieces = [None] * A_HEADS
    for kvh in range(A_KV_HEADS):
        for g in range(group):
            hq = group * kvh + g
            og = outs[kvh][g * tq:(g + 1) * tq]
            if hq % 2 != kvh % 2:
                og = pltpu.roll(og, HEAD_DIM, 1)
            pieces[hq] = og
    slabs = [jnp.where(lo_half, pieces[2 * s], pieces[2 * s + 1]) for s in range(A_HEADS // 2)]
    return jnp.concatenate(slabs, axis=1)


def _attn_project(x, g_ref, wa_ref):
    h = _rms(x, g_ref[...])
    proj = _mm(h, wa_ref[...])
    q = proj[:, :A_WIDTH] * ATTN_SCALE
    k = proj[:, A_WIDTH:A_WIDTH + KV_WIDTH]
    v = proj[:, A_WIDTH + KV_WIDTH:A_WIDTH + 2 * KV_WIDTH]
    ga = proj[:, A_WIDTH + 2 * KV_WIDTH:ATTN_COLS]
    ma = proj[:, ATTN_COLS:ATTN_COLS + D_MODEL]
    mb = proj[:, ATTN_COLS + D_MODEL:]
    return q, k, v, ga, ma, mb


def _merge_tail(x, o, ga, ma, mb, yb, woa_ref, wo_ref, fg_ref, final):
    ya = _mm(o * _silu(ga), woa_ref[...])
    merged = _sigmoid(ma) * ya + _sigmoid(mb) * yb
    out = x + _mm(merged, wo_ref[...])
    return _rms(out, fg_ref[...]) if final else out


def _attn_prompt_kernel(x_ref, ye_ref, yo_ref, g_ref, wa_ref, woa_ref, wo_ref, fg_ref, relb_ref, sinks_ref,
                        out_ref, kw_ref, vw_ref, kprev, vprev, bprev, bcur, *, final):
    b = pl.program_id(0)
    n = pl.program_id(1)

    @pl.when((b == 0) & (n == 0))
    def _():
        _init_bias(relb_ref, bprev, bcur)

    @pl.when(n == 0)
    def _():
        kprev[...] = jnp.zeros_like(kprev)
        vprev[...] = jnp.zeros_like(vprev)

    x = x_ref[0]
    q, k, v, ga, ma, mb = _attn_project(x, g_ref, wa_ref)
    kp, vp = kprev[...], vprev[...]
    os = []
    for j in range(ATTN_BLOCKS):
        rows = slice(WINDOW * j, WINDOW * (j + 1))
        os.append(_attn_core(q[rows], kp, vp, k[rows], v[rows], bprev, bcur, sinks_ref, WINDOW,
                             (n == 0) if j == 0 else None))
        kp, vp = k[rows], v[rows]
    kprev[...] = kp
    vprev[...] = vp
    kw_ref[0] = kp
    vw_ref[0] = vp
    o = jnp.concatenate(os, axis=0)
    yb = jnp.concatenate([(ye_ref if j % 2 == 0 else yo_ref)[0, j // 2] for j in range(ATTN_BLOCKS)], axis=0)
    out_ref[0] = _merge_tail(x, o, ga, ma, mb, yb, woa_ref, wo_ref, fg_ref, final)


def _attn_sample_kernel(x_ref, yb_ref, g_ref, wa_ref, woa_ref, wo_ref, fg_ref, relb_ref, sinks_ref, kc_ref, vc_ref,
                        out_ref, ko_ref, vo_ref, qbuf, kbuf, vbuf, obuf, bprev, bcur, *, seq, final):
    @pl.when(pl.program_id(0) == 0)
    def _():
        _init_bias(relb_ref, bprev, bcur)

    x = x_ref[...]
    q, k, v, ga, ma, mb = _attn_project(x, g_ref, wa_ref)
    qbuf[...] = q
    kbuf[...] = k
    vbuf[...] = v
    pad = jnp.zeros((WINDOW - seq, KV_WIDTH), F32)

    def body(j, carry):
        rows = pl.ds(pl.multiple_of(j * seq, seq), seq)
        kn = kbuf[rows, :]
        vn = vbuf[rows, :]
        kcj = kc_ref[j]
        vcj = vc_ref[j]
        o = _attn_core(qbuf[rows, :], kcj, vcj, jnp.concatenate([kn, pad], axis=0),
                       jnp.concatenate([vn, pad], axis=0), bprev, bcur, sinks_ref, seq, None)
        obuf[rows, :] = o
        ko_ref[j, pl.ds(0, WINDOW - seq), :] = kcj[seq:, :]
        ko_ref[j, pl.ds(WINDOW - seq, seq), :] = kn
        vo_ref[j, pl.ds(0, WINDOW - seq), :] = vcj[seq:, :]
        vo_ref[j, pl.ds(WINDOW - seq, seq), :] = vn
        return carry

    lax.fori_loop(0, ROWS // seq, body, 0, unroll=SEQ_UNROLL)
    out_ref[...] = _merge_tail(x, obuf[...], ga, ma, mb, yb_ref[...], woa_ref, wo_ref, fg_ref, final)


def _head_sums(x, ones_blocks, terms=2):
    outs = []
    width = ones_blocks.shape[0]
    for i in range(x.shape[1] // width):
        xs = x[:, width * i:width * (i + 1)]
        hi = _bf(xs)
        acc = jnp.dot(hi, ones_blocks, preferred_element_type=F32)
        if terms == 2:
            acc = acc + jnp.dot(_bf(xs - hi.astype(F32)), ones_blocks, preferred_element_type=F32)
        outs.append(acc)
    return jnp.concatenate(outs, axis=1)


def _seg_cumsum(x, seq):
    pos = lax.broadcasted_iota(jnp.int32, (x.shape[0], 1), 0) & (seq - 1)
    s = 1
    while s < seq:
        if s % 8 == 0 and seq == x.shape[0]:
            x = jnp.concatenate([x[:s], x[s:] + x[:-s]], axis=0)
        else:
            x = x + jnp.where(pos >= s, pltpu.roll(x, s, 0), 0.0)
        s *= 2
    return x


def _neumann_levels(ps, seq, out):
    n = ps[0].shape[0]
    ri = lax.broadcasted_iota(jnp.int32, (n, n), 0)
    ci = lax.broadcasted_iota(jnp.int32, (n, n), 1)
    eye = jnp.where(ri == ci, 1.0, 0.0)
    xs = [eye + p for p in ps]
    if seq > 2:
        pws = [_mm(p, p) for p in ps]
        yield
        span = 2
        while span < seq:
            last = 2 * span >= seq
            skip = span if span >= 16 else 0
            nxt_x, nxt_p = [], []
            for pw, x in zip(pws, xs):
                rhs = x if last else jnp.concatenate([pw, x], axis=1)
                res = _mm(pw[skip:], rhs)
                dx = res if last else res[:, n:]
                if skip:
                    nxt_x.append(jnp.concatenate([x[:skip], x[skip:] + dx], axis=0))
                else:
                    nxt_x.append(x + dx)
                if not last:
                    pn = res[:, :n]
                    nxt_p.append(jnp.concatenate([jnp.zeros((skip, n), F32), pn], axis=0) if skip else pn)
            xs, pws = nxt_x, nxt_p
            span *= 2
            yield
    out[:] = xs


PREP_NAMES = ("at", "rt", "bkt", "v", "gt", "bonus", "sgb", "bkh")
RWKV_WEIGHT_NAMES = ("g", "wr", "wob", "mu", "w0", "lora", "a0", "kk", "ka", "rk", "lng", "lnb")
PREP_SLAB = 2 * LANES


def _ones_blocks():
    ri = lax.broadcasted_iota(jnp.int32, (2 * LANES, 2 * LANES), 0)
    ci = lax.broadcasted_iota(jnp.int32, (2 * LANES, 2 * LANES), 1)
    return jnp.where((ri >> 6) == (ci >> 6), 1.0, 0.0).astype(BF16)


def _rwkv_prep(x, first_ref, w, seq, out, ps_ref, lora_ref):
    nseq = ROWS // seq
    h = _rms(x, w["g"][...])
    proj = _mm(h, w["wr"][...])
    ps_ref[...] = proj[:, :SHIFT_W]
    out["sgb"][...] = _silu(proj[:, SHIFT_W:])
    pos = lax.broadcasted_iota(jnp.int32, (ROWS, 1), 0) & (seq - 1)

    def shifted(cols):
        cur = ps_ref[:, cols]
        prev = jnp.where(pos == 0, first_ref[:, cols], pltpu.roll(cur, 1, 0))
        return cur + (prev - cur) * w["mu"][:, cols]

    zl = shifted(slice(3 * B_WIDTH, SHIFT_W))
    lo_half = lax.broadcasted_iota(jnp.int32, (1, LANES), 1) < HEAD_DIM
    lora_ref[...] = _mm(jnp.where(lo_half, jnp.tanh(zl), zl), w["lora"][...])
    yield
    ones_blocks = _ones_blocks()
    for d in range(B_WIDTH // PREP_SLAB):
        cols = slice(PREP_SLAB * d, PREP_SLAB * (d + 1))

        def wcols(name):
            return w[name][:, cols]

        r = shifted(cols)
        k = shifted(slice(B_WIDTH + cols.start, B_WIDTH + cols.stop))
        v = shifted(slice(2 * B_WIDTH + cols.start, 2 * B_WIDTH + cols.stop))
        lw = -math.exp(-0.5) * _sigmoid(wcols("w0") + lora_ref[:, cols])
        asig = _sigmoid(wcols("a0") + lora_ref[:, slice(B_WIDTH + cols.start, B_WIDTH + cols.stop)])
        kkr = k * wcols("kk")
        kk = kkr * lax.rsqrt(jnp.maximum(_head_sums(kkr * kkr, ones_blocks), 1e-24))
        kp = k * (1.0 + (asig - 1.0) * wcols("ka"))
        a_ = -kk
        b_ = kk * asig
        cum = _seg_cumsum(lw, seq)
        if nseq == 1:
            tot = jnp.broadcast_to(cum[ROWS - 1:ROWS, :], cum.shape)
        else:
            c3 = cum.reshape(nseq, seq, PREP_SLAB)
            tot = jnp.broadcast_to(c3[:, seq - 1:seq, :], c3.shape).reshape(cum.shape)
        g_inv = jnp.exp(-cum)
        g_last = jnp.exp(tot - cum)
        out["at"][:, cols] = a_ * jnp.exp(cum - lw)
        out["rt"][:, cols] = r * jnp.exp(cum)
        bt = b_ * g_inv
        kt = kp * g_inv
        bh = b_ * g_last
        kh = kp * g_last
        for q in range(PREP_SLAB // LANES):
            ql = slice(LANES * q, LANES * (q + 1))
            pair = PREP_SLAB // LANES * d + q
            out["bkt"][pair] = _bf(jnp.concatenate([bt[:, ql], kt[:, ql]], axis=0).T)
            out["bkh"][pair] = _bf(jnp.concatenate([bh[:, ql], kh[:, ql]], axis=0))
        out["gt"][:, cols] = jnp.exp(tot)
        out["v"][:, cols] = v
        out["bonus"][:, cols] = _head_sums(r * kp * wcols("rk"), ones_blocks) * v
        yield


def _rwkv_core(buf, w, seq, av_ref, t_ref, arbk_ref, res, *, s_ref=None, si_ref=None, so_ref=None,
               x0_ref=None, y0_ref=None, uvt_ref=None):
    nseq = ROWS // seq
    prompt = s_ref is not None
    lo_half = lax.broadcasted_iota(jnp.int32, (1, LANES), 1) < HEAD_DIM
    hi_half = jnp.logical_not(lo_half)
    rr = lax.broadcasted_iota(jnp.int32, (ROWS, ROWS), 0)
    cc = lax.broadcasted_iota(jnp.int32, (ROWS, ROWS), 1)
    shift = int(math.log2(seq))
    same = (rr >> shift) == (cc >> shift)
    strict = same & (rr > cc)
    incl = same & (rr >= cc)
    bd = (rr < HEAD_DIM) == (cc < HEAD_DIM)
    pair_slices = [slice(LANES * p, LANES * (p + 1)) for p in range(N_PAIRS)]
    at_ref, rt_ref, bkt_ref, v_ref, gt_ref, bkh_ref = (buf[n] for n in ("at", "rt", "bkt", "v", "gt", "bkh"))

    def halves(t):
        return jnp.concatenate([jnp.where(lo_half, t, 0.0), jnp.where(hi_half, t, 0.0)], axis=0)

    def blockdiag(s_nat):
        return jnp.concatenate([jnp.where(lo_half, s_nat, 0.0), jnp.where(lo_half, 0.0, s_nat)], axis=0)

    if prompt:
        s0s = [s_ref[p] for p in range(N_PAIRS)]
        s0ts = [_bf(s0.T) for s0 in s0s]
        x0s = [_mm(at_ref[:, sl], s0t) for sl, s0t in zip(pair_slices, s0ts)]
        y0s = [_mm(rt_ref[:, sl], s0t) for sl, s0t in zip(pair_slices, s0ts)]
    else:
        def read_body(j, carry):
            rows = pl.ds(pl.multiple_of(j * seq, seq), seq)
            for p, sl in enumerate(pair_slices):
                s0 = blockdiag(si_ref[j, p])
                arj = jnp.concatenate([at_ref[rows, sl], rt_ref[rows, sl]], axis=0)
                xy = _mm_nt(arj, s0)
                x0_ref[rows, sl] = xy[:seq]
                y0_ref[rows, sl] = xy[seq:]
            return carry
        lax.fori_loop(0, nseq, read_body, 0, unroll=SEQ_UNROLL)
        x0s = [x0_ref[:, sl] for sl in pair_slices]
        y0s = [y0_ref[:, sl] for sl in pair_slices]
    yield

    a_abs, a_aks = [], []
    for p, sl in enumerate(pair_slices):
        rbk = []
        for hh in range(2):
            half = lo_half if hh == 0 else hi_half
            ga = _mm(jnp.where(half, at_ref[:, sl], 0.0), bkt_ref[p])
            gr = _mm(jnp.where(half, rt_ref[:, sl], 0.0), bkt_ref[p])
            a_abs.append(jnp.where(strict, ga[:, :ROWS], 0.0))
            a_aks.append(jnp.where(strict, ga[:, ROWS:], 0.0))
            rbk.append(jnp.where(incl, gr[:, :ROWS], 0.0))
            rbk.append(jnp.where(incl, gr[:, ROWS:], 0.0))
        arbk_ref[p] = _bf(jnp.concatenate(rbk, axis=1))
        av_ref[:, sl] = _mm(jnp.concatenate(a_aks[2 * p:2 * p + 2], axis=1), halves(v_ref[:, sl]))
        if p % 4 == 3:
            yield
    ts = []
    yield from _neumann_levels(a_abs, seq, ts)
    for p in range(N_PAIRS):
        t_ref[p] = _bf(jnp.concatenate(ts[2 * p:2 * p + 2], axis=1))

    vss = [v_ref[:, sl] for sl in pair_slices]
    us = [jnp.dot(t_ref[p], _bf(halves(x0s[p] + av_ref[:, sl])), preferred_element_type=F32)
          for p, sl in enumerate(pair_slices)]
    yield
    ys = []
    for p in range(N_PAIRS):
        u, vs = us[p], vss[p]
        rhs = jnp.concatenate([jnp.where(lo_half, u, 0.0), jnp.where(lo_half, vs, 0.0),
                               jnp.where(hi_half, u, 0.0), jnp.where(hi_half, vs, 0.0)], axis=0)
        ys.append(y0s[p] + jnp.dot(arbk_ref[p], _bf(rhs), preferred_element_type=F32))
    yield

    uvts = [jnp.concatenate([us[p], vss[p]], axis=0).T for p in range(N_PAIRS)]
    if prompt:
        s1s = [s0s[p] * gt_ref[0:1, sl] + jnp.where(bd, _mm(uvts[p], bkh_ref[p]), 0.0)
               for p, sl in enumerate(pair_slices)]
        for p in range(N_PAIRS):
            s_ref[p] = s1s[p]
        res["s1"] = s1s
    else:
        colseq = (lax.broadcasted_iota(jnp.int32, (1, 2 * ROWS), 1) & (ROWS - 1)) >> shift
        for p in range(N_PAIRS):
            uvt_ref[p] = _bf(uvts[p])

        def upd_body(j, carry):
            row = pl.ds(pl.multiple_of(j * seq, seq), 1)
            for p, sl in enumerate(pair_slices):
                uvt = jnp.where(colseq == j, uvt_ref[p], jnp.zeros((), BF16))
                upd = jnp.dot(uvt, bkh_ref[p], preferred_element_type=F32)
                so_ref[j, p] = (si_ref[j, p] * gt_ref[row, sl]
                                + jnp.where(lo_half, upd[:HEAD_DIM], upd[HEAD_DIM:]))
            return carry
        lax.fori_loop(0, nseq, upd_body, 0, unroll=SEQ_UNROLL)
    yield

    ones_blocks = _ones_blocks()
    y = jnp.concatenate(ys, axis=1)
    mean = _head_sums(y, ones_blocks, terms=1) * (1.0 / HEAD_DIM)
    dlt = y - mean
    var = _head_sums(dlt * dlt, ones_blocks, terms=1) * (1.0 / HEAD_DIM)
    yn = dlt * lax.rsqrt(var + GN_EPS) * w["lng"][...] + w["lnb"][...]
    yo = (yn + buf["bonus"][...]) * buf["sgb"][...]
    res["yb"] = _mm(yo, w["wob"][...])


N_RWKV_W = len(RWKV_WEIGHT_NAMES)
N_PREP = len(PREP_NAMES)
PROMPT_ORDER = "CCPCCPCCPCCPCCPCCC"


def _rwkv_prompt_kernel(*refs, tiles_per_seq, n_tiles):
    x_ref = refs[0]
    w = dict(zip(RWKV_WEIGHT_NAMES, refs[1:1 + N_RWKV_W]))
    ye_ref, yo_ref, so_ref, sh_ref = refs[1 + N_RWKV_W:5 + N_RWKV_W]
    scratch = refs[5 + N_RWKV_W:]
    set_a = dict(zip(PREP_NAMES, scratch[:N_PREP]))
    set_b = dict(zip(PREP_NAMES, scratch[N_PREP:2 * N_PREP]))
    av_ref, lora_ref, t_ref, arbk_ref, s_ref, ps_ref, carry_ref = scratch[2 * N_PREP:]
    j = pl.program_id(0)
    starts_seq = lax.rem(2 * j, tiles_per_seq) == 0

    @pl.when(j == 0)
    def _():
        for name in PREP_NAMES:
            set_b[name][...] = jnp.zeros_like(set_b[name])
        s_ref[...] = jnp.zeros_like(s_ref)

    @pl.when(starts_seq)
    def _():
        carry_ref[...] = jnp.zeros_like(carry_ref)

    def half_step(x, prepared, current):
        res = {}
        prep = _rwkv_prep(x, carry_ref, w, ROWS, prepared, ps_ref, lora_ref)
        core = _rwkv_core(current, w, ROWS, av_ref, t_ref, arbk_ref, res, s_ref=s_ref)
        for kind in PROMPT_ORDER:
            next(prep if kind == "P" else core, None)
        for _ in prep:
            pass
        for _ in core:
            pass
        carry_ref[...] = ps_ref[ROWS - 1:ROWS, :]
        return res

    res = half_step(x_ref[0, 0:ROWS, :], set_a, set_b)
    yo_ref[0, 0] = res["yb"]

    @pl.when(starts_seq & (j > 0))
    def _():
        lo_half = lax.broadcasted_iota(jnp.int32, (1, LANES), 1) < HEAD_DIM
        for p in range(N_PAIRS):
            so_ref[0, p] = jnp.where(lo_half, res["s1"][p][:HEAD_DIM], res["s1"][p][HEAD_DIM:])

    @pl.when(starts_seq)
    def _():
        s_ref[...] = jnp.zeros_like(s_ref)

    res = half_step(x_ref[0, ROWS:2 * ROWS, :], set_b, set_a)
    sh_ref[0] = ps_ref[ROWS - 1:ROWS, :]

    @pl.when(2 * j < n_tiles)
    def _():
        ye_ref[0, 0] = res["yb"]


def _rwkv_sample_kernel(*refs, seq):
    x_ref = refs[0]
    w = dict(zip(RWKV_WEIGHT_NAMES, refs[1:1 + N_RWKV_W]))
    si_ref, shin_ref, yb_ref, so_ref, ps_ref = refs[1 + N_RWKV_W:6 + N_RWKV_W]
    scratch = refs[6 + N_RWKV_W:]
    buf = dict(zip(PREP_NAMES, scratch[:N_PREP]))
    av_ref, lora_ref, t_ref, arbk_ref, x0_ref, y0_ref, uvt_ref = scratch[N_PREP:]
    res = {}
    for _ in _rwkv_prep(x_ref[...], shin_ref, w, seq, buf, ps_ref, lora_ref):
        pass
    for _ in _rwkv_core(buf, w, seq, av_ref, t_ref, arbk_ref, res, si_ref=si_ref, so_ref=so_ref,
                        x0_ref=x0_ref, y0_ref=y0_ref, uvt_ref=uvt_ref):
        pass
    yb_ref[...] = res["yb"]


def _const_spec(shape):
    nd = len(shape)
    return pl.BlockSpec(shape, lambda *_: (0,) * nd, pipeline_mode=pl.Buffered(1))


def _smem_spec():
    return pl.BlockSpec(memory_space=pltpu.SMEM)


def _params(n_axes):
    return pltpu.CompilerParams(dimension_semantics=("arbitrary",) * n_axes, vmem_limit_bytes=VMEM_LIMIT)


def _attn_weight_specs():
    return [_const_spec((1, D_MODEL)), _const_spec((D_MODEL, ATTN_COLS + MERGE_COLS)),
            _const_spec((A_WIDTH, D_MODEL)), _const_spec((D_MODEL, D_MODEL)), _const_spec((1, D_MODEL)),
            _smem_spec(), _smem_spec()]


def _attention_prompt(x, yb_even, yb_odd, weights, final):
    bsz, t, _ = x.shape
    rows = ATTN_BLOCKS * WINDOW
    nb = t // rows
    tile = pl.BlockSpec((1, rows, D_MODEL), lambda b, n: (b, n, 0))
    half = pl.BlockSpec((1, ATTN_BLOCKS // 2, ROWS, D_MODEL), lambda b, n: (b, n, 0, 0))
    win = pl.BlockSpec((1, WINDOW, KV_WIDTH), lambda b, n: (b, 0, 0))
    return pl.pallas_call(
        functools.partial(_attn_prompt_kernel, final=final),
        grid=(bsz, nb),
        in_specs=[tile, half, half] + _attn_weight_specs(),
        out_specs=[tile, win, win],
        out_shape=[jax.ShapeDtypeStruct((bsz, t, D_MODEL), F32),
                   jax.ShapeDtypeStruct((bsz, WINDOW, KV_WIDTH), F32),
                   jax.ShapeDtypeStruct((bsz, WINDOW, KV_WIDTH), F32)],
        scratch_shapes=[pltpu.VMEM((WINDOW, KV_WIDTH), F32), pltpu.VMEM((WINDOW, KV_WIDTH), F32),
                        pltpu.VMEM((A_HEADS, WINDOW, WINDOW), F32), pltpu.VMEM((A_HEADS, WINDOW, WINDOW), F32)],
        compiler_params=_params(2),
        name="attn_prompt",
    )(x, yb_even, yb_odd, *weights)


def _attention_sample(x2d, yb, seq, weights, kc, vc, final):
    rows = x2d.shape[0]
    nseq = ROWS // seq
    tile = pl.BlockSpec((ROWS, D_MODEL), lambda i: (i, 0))
    cache = pl.BlockSpec((nseq, WINDOW, KV_WIDTH), lambda i: (i, 0, 0))
    return pl.pallas_call(
        functools.partial(_attn_sample_kernel, seq=seq, final=final),
        grid=(rows // ROWS,),
        in_specs=[tile, tile] + _attn_weight_specs() + [cache, cache],
        out_specs=[tile, cache, cache],
        out_shape=[jax.ShapeDtypeStruct((rows, D_MODEL), F32),
                   jax.ShapeDtypeStruct(kc.shape, F32), jax.ShapeDtypeStruct(vc.shape, F32)],
        scratch_shapes=[pltpu.VMEM((ROWS, A_WIDTH), F32), pltpu.VMEM((ROWS, KV_WIDTH), F32),
                        pltpu.VMEM((ROWS, KV_WIDTH), F32), pltpu.VMEM((ROWS, A_WIDTH), F32),
                        pltpu.VMEM((A_HEADS, WINDOW, WINDOW), F32), pltpu.VMEM((A_HEADS, WINDOW, WINDOW), F32)],
        compiler_params=_params(1),
        name="attn_sample",
    )(x2d, yb, *weights, kc, vc)


def _prep_scratch(names):
    shapes = {"bkt": pltpu.VMEM((N_PAIRS, LANES, 2 * ROWS), BF16), "bkh": pltpu.VMEM((N_PAIRS, 2 * ROWS, LANES), BF16)}
    return [shapes.get(n, pltpu.VMEM((ROWS, B_WIDTH), F32)) for n in names]


def _core_scratch():
    return [pltpu.VMEM((ROWS, B_WIDTH), F32), pltpu.VMEM((ROWS, 2 * B_WIDTH), F32),
            pltpu.VMEM((N_PAIRS, ROWS, 2 * ROWS), BF16), pltpu.VMEM((N_PAIRS, ROWS, 4 * ROWS), BF16)]


def _rwkv_weight_specs():
    row = _const_spec((1, B_WIDTH))
    return [_const_spec((1, D_MODEL)), _const_spec((D_MODEL, RWKV_COLS)), _const_spec((B_WIDTH, D_MODEL)),
            _const_spec((1, SHIFT_W)), row, _const_spec((LANES, 2 * B_WIDTH)), row, row, row, row, row, row]


def _rwkv_prompt(x, weights):
    bsz, t, _ = x.shape
    nb = t // ROWS
    npair = nb // 2
    total = bsz * npair

    def pair_of(jj):
        return lax.div(jj, npair), lax.rem(jj, npair)

    def x_map(j):
        return (*pair_of(jnp.minimum(j, total - 1)), 0)

    def even_map(j):
        return (*pair_of(jnp.minimum(j, total - 1)), 0, 0)

    def odd_map(j):
        return (*pair_of(jnp.maximum(j - 1, 0)), 0, 0)

    ytile = jax.ShapeDtypeStruct((bsz, npair, ROWS, D_MODEL), F32)
    return pl.pallas_call(
        functools.partial(_rwkv_prompt_kernel, tiles_per_seq=nb, n_tiles=bsz * nb),
        grid=(total + 1,),
        in_specs=[pl.BlockSpec((1, 2 * ROWS, D_MODEL), x_map)] + _rwkv_weight_specs(),
        out_specs=[pl.BlockSpec((1, 1, ROWS, D_MODEL), even_map),
                   pl.BlockSpec((1, 1, ROWS, D_MODEL), odd_map),
                   pl.BlockSpec((1, N_PAIRS, HEAD_DIM, LANES), lambda j: (odd_map(j)[0], 0, 0, 0)),
                   pl.BlockSpec((1, 1, SHIFT_W), lambda j: (x_map(j)[0], 0, 0))],
        out_shape=[ytile, ytile,
                   jax.ShapeDtypeStruct((bsz, N_PAIRS, HEAD_DIM, LANES), F32),
                   jax.ShapeDtypeStruct((bsz, 1, SHIFT_W), F32)],
        scratch_shapes=_prep_scratch(PREP_NAMES) + _prep_scratch(PREP_NAMES) + _core_scratch()
        + [pltpu.VMEM((N_PAIRS, LANES, LANES), F32), pltpu.VMEM((ROWS, SHIFT_W), F32),
           pltpu.VMEM((1, SHIFT_W), F32)],
        compiler_params=_params(1),
        name="rwkv_prompt",
    )(x, *weights)


def _rwkv_sample(x2d, seq, weights, s_nat, shift_rows):
    rows = x2d.shape[0]
    nseq = ROWS // seq
    tile = pl.BlockSpec((ROWS, D_MODEL), lambda i: (i, 0))
    state = pl.BlockSpec((nseq, N_PAIRS, HEAD_DIM, LANES), lambda i: (i, 0, 0, 0))
    shift = pl.BlockSpec((ROWS, SHIFT_W), lambda i: (i, 0))
    wide = pltpu.VMEM((ROWS, B_WIDTH), F32)
    return pl.pallas_call(
        functools.partial(_rwkv_sample_kernel, seq=seq),
        grid=(rows // ROWS,),
        in_specs=[tile] + _rwkv_weight_specs() + [state, shift],
        out_specs=[tile, state, shift],
        out_shape=[jax.ShapeDtypeStruct((rows, D_MODEL), F32),
                   jax.ShapeDtypeStruct(s_nat.shape, F32),
                   jax.ShapeDtypeStruct((rows, SHIFT_W), F32)],
        scratch_shapes=_prep_scratch(PREP_NAMES) + _core_scratch()
        + [wide, wide, pltpu.VMEM((N_PAIRS, ROWS, 2 * ROWS), BF16)],
        compiler_params=_params(1),
        name="rwkv_sample",
    )(x2d, *weights, s_nat, shift_rows)


def _pairs_from_heads(s):
    b = s.shape[0]
    return s.reshape(b, N_PAIRS, 2, HEAD_DIM, HEAD_DIM).transpose(0, 1, 3, 2, 4).reshape(b, N_PAIRS, HEAD_DIM, LANES)


def _heads_from_pairs(s):
    b = s.shape[0]
    return s.reshape(b, N_PAIRS, HEAD_DIM, 2, HEAD_DIM).transpose(0, 1, 3, 2, 4).reshape(b, B_HEADS, HEAD_DIM, HEAD_DIM)


def kernel(x_prompt, x_sample, cache_k_win, cache_v_win, state_wkv, state_shift, rel_bias, norm_g, w_in, attn_sinks, shift_mu, rwkv_w0, rwkv_w2, rwkv_a0, rwkv_a2, rwkv_k_k, rwkv_k_a, rwkv_r_k, lnx_g, lnx_b, w_out_a, w_out_b, w_o, final_g):
    depth = w_in.shape[0]
    bsz, t, _ = x_prompt.shape
    dbsz, dseq, _ = x_sample.shape
    hp = x_prompt
    hs = x_sample.reshape(dbsz * dseq, D_MODEL)
    fg = final_g.reshape(1, D_MODEL)
    outs = [[] for _ in range(8)]
    for l in range(depth):
        g = norm_g[l].reshape(1, D_MODEL)
        w = w_in[l]
        wa = _bf(jnp.concatenate([w[:, :ATTN_COLS], w[:, ATTN_COLS + RWKV_COLS:]], axis=1))
        wr = _bf(w[:, ATTN_COLS:ATTN_COLS + RWKV_COLS])
        woa, wob, wo = _bf(w_out_a[l]), _bf(w_out_b[l]), _bf(w_o[l])
        zeros = jnp.zeros((DECAY_LORA, B_WIDTH), F32)
        lora = _bf(jnp.concatenate([jnp.concatenate([rwkv_w2[l], zeros], axis=1),
                                    jnp.concatenate([zeros, rwkv_a2[l]], axis=1)], axis=0))
        rw = [g, wr, wob, shift_mu[l].reshape(1, SHIFT_W), rwkv_w0[l].reshape(1, B_WIDTH), lora,
              rwkv_a0[l].reshape(1, B_WIDTH), rwkv_k_k[l].reshape(1, B_WIDTH), rwkv_k_a[l].reshape(1, B_WIDTH),
              rwkv_r_k[l].reshape(1, B_WIDTH), lnx_g[l].reshape(1, B_WIDTH), lnx_b[l].reshape(1, B_WIDTH)]
        aw = [g, wa, woa, wo, fg, rel_bias, attn_sinks[l]]
        final = l == depth - 1

        yb_even, yb_odd, s1, t1 = _rwkv_prompt(hp, rw)
        hp, k1, v1 = _attention_prompt(hp, yb_even, yb_odd, aw, final)

        kc = cache_k_win[l].reshape(dbsz, WINDOW, KV_WIDTH)
        vc = cache_v_win[l].reshape(dbsz, WINDOW, KV_WIDTH)
        shift_rows = jnp.repeat(state_shift[l], dseq, axis=0)
        yb_s, s2, ps_s = _rwkv_sample(hs, dseq, rw, _pairs_from_heads(state_wkv[l]), shift_rows)
        hs, k2, v2 = _attention_sample(hs, yb_s, dseq, aw, kc, vc, final)

        outs[0].append(k1.reshape(bsz, WINDOW, A_KV_HEADS, HEAD_DIM))
        outs[1].append(v1.reshape(bsz, WINDOW, A_KV_HEADS, HEAD_DIM))
        outs[2].append(_heads_from_pairs(s1))
        outs[3].append(t1.reshape(bsz, SHIFT_W))
        outs[4].append(k2.reshape(dbsz, WINDOW, A_KV_HEADS, HEAD_DIM))
        outs[5].append(v2.reshape(dbsz, WINDOW, A_KV_HEADS, HEAD_DIM))
        outs[6].append(_heads_from_pairs(s2))
        outs[7].append(ps_s.reshape(dbsz, dseq, SHIFT_W)[:, -1])
    y_prompt = hp
    y_sample = hs.reshape(dbsz, dseq, D_MODEL)
    return (y_prompt, y_sample) + tuple(jnp.stack(o) for o in outs)
```

```python
import functools
import math

import numpy as np
import jax
import jax.numpy as jnp
from jax import lax
from jax.experimental import pallas as pl
from jax.experimental.pallas import tpu as pltpu

D_MODEL = 1024
HEAD_DIM = 64
A_HEADS = 16
A_KV_HEADS = 4
A_WIDTH = A_HEADS * HEAD_DIM
KV_WIDTH = A_KV_HEADS * HEAD_DIM
WINDOW = 128
ATTN_SCALE = HEAD_DIM ** -0.5
N_BUCKETS = 32
MAX_DISTANCE = 128
B_HEADS = 16
B_WIDTH = B_HEADS * HEAD_DIM
DECAY_LORA = 64
A_LORA = 64
SHIFT_W = 3 * B_WIDTH + DECAY_LORA + A_LORA
GN_EPS = 64e-5
NORM_EPS = 1e-6

ATTN_COLS = 2 * A_WIDTH + 2 * KV_WIDTH
RWKV_COLS = SHIFT_W + B_WIDTH
MERGE_COLS = 2 * D_MODEL

ROWS = 128
LANES = 128
N_PAIRS = B_HEADS // 2
ATTN_BLOCKS = 4
SEQ_UNROLL = 4
NEG = -1e30
VMEM_LIMIT = 56 * 1024 * 1024

F32 = jnp.float32
BF16 = jnp.bfloat16


def _bucket_ranges():
    d = np.arange(0, WINDOW + 1)
    max_exact = N_BUCKETS // 2
    df = np.maximum(d, 1).astype(np.float32)
    large = max_exact + (np.log(df / np.float32(max_exact)) / np.float32(math.log(MAX_DISTANCE / max_exact))
                         * np.float32(N_BUCKETS - max_exact)).astype(np.int32)
    large = np.minimum(large, N_BUCKETS - 1)
    bucket = np.where(d < max_exact, d, large)
    out = []
    for b in range(N_BUCKETS):
        idx = np.nonzero(bucket == b)[0]
        if idx.size:
            assert idx[-1] - idx[0] + 1 == idx.size
            out.append((b, int(idx[0]), int(idx[-1])))
    return tuple(out)


_BUCKET_RANGES = _bucket_ranges()


def _bf(x):
    return x.astype(BF16)


def _mm(a, b):
    return jnp.dot(_bf(a), _bf(b), preferred_element_type=F32)


def _mm_nt(a, b):
    return lax.dot_general(_bf(a), _bf(b), (((1,), (1,)), ((), ())), preferred_element_type=F32)


def _rms(x, g):
    ms = jnp.mean(x * x, axis=-1, keepdims=True)
    return x * lax.rsqrt(ms + NORM_EPS) * g


def _sigmoid(x):
    return 1.0 / (1.0 + jnp.exp(-x))


def _silu(x):
    return x * _sigmoid(x)


def _init_bias(relb_ref, bprev_ref, bcur_ref):
    qi = lax.broadcasted_iota(jnp.int32, (WINDOW, WINDOW), 0)
    kj = lax.broadcasted_iota(jnp.int32, (WINDOW, WINDOW), 1)
    delta = qi - kj

    def body(h, carry):
        bp = jnp.full((WINDOW, WINDOW), NEG, F32)
        bc = jnp.full((WINDOW, WINDOW), NEG, F32)
        for (b, lo, hi) in _BUCKET_RANGES:
            val = relb_ref[b, h]
            bp = jnp.where((delta >= lo - WINDOW) & (delta <= hi - WINDOW), val, bp)
            bc = jnp.where((delta >= lo) & (delta <= hi), val, bc)
        bprev_ref[h] = bp
        bcur_ref[h] = bc
        return carry

    lax.fori_loop(0, A_HEADS, body, 0)


def _attn_core(q, kp, vp, kc, vc, bprev_ref, bcur_ref, sinks_ref, tq, first):
    group = A_HEADS // A_KV_HEADS
    lo_half = lax.broadcasted_iota(jnp.int32, (1, LANES), 1) < HEAD_DIM
    hi_half = jnp.logical_not(lo_half)
    rowi = lax.broadcasted_iota(jnp.int32, (group * tq, 1), 0)
    sps, scs, sinks = [], [], []
    for kvh in range(A_KV_HEADS):
        i, c = divmod(kvh, 2)
        sl = slice(LANES * i, LANES * (i + 1))
        rows = []
        for g in range(group):
            hq = group * kvh + g
            qs = q[:, LANES * (hq // 2):LANES * (hq // 2 + 1)]
            qm = jnp.where(lo_half if hq % 2 == 0 else hi_half, qs, 0.0)
            if hq % 2 != c:
                qm = pltpu.roll(qm, HEAD_DIM, 1)
            rows.append(qm)
        lhs = _bf(jnp.concatenate(rows, axis=0))
        bp = bprev_ref[group * kvh:group * (kvh + 1), 0:tq, :].reshape(group * tq, WINDOW)
        bc = bcur_ref[group * kvh:group * (kvh + 1), 0:tq, :].reshape(group * tq, WINDOW)
        sp = _mm_nt(lhs, kp[:, sl]) + bp
        if first is not None:
            sp = jnp.where(first, NEG, sp)
        sps.append(sp)
        scs.append(_mm_nt(lhs, kc[:, sl]) + bc)
        sink = sinks_ref[group * kvh + group - 1]
        for g in range(group - 2, -1, -1):
            sink = jnp.where(rowi < (g + 1) * tq, sinks_ref[group * kvh + g], sink)
        sinks.append(sink)
    ms = [jnp.maximum(jnp.max(jnp.maximum(sp, sc), axis=-1, keepdims=True), sink)
          for sp, sc, sink in zip(sps, scs, sinks)]
    pps = [jnp.exp(sp - m) for sp, m in zip(sps, ms)]
    pcs = [jnp.exp(sc - m) for sc, m in zip(scs, ms)]
    outs = []
    for kvh in range(A_KV_HEADS):
        i, c = divmod(kvh, 2)
        sl = slice(LANES * i, LANES * (i + 1))
        own = lo_half if c == 0 else hi_half
        o = _mm(pps[kvh], jnp.where(own, vp[:, sl], 1.0)) + _mm(pcs[kvh], jnp.where(own, vc[:, sl], 1.0))
        den = pltpu.roll(o, HEAD_DIM, 1) + jnp.exp(sinks[kvh] - ms[kvh])
        outs.append(o / den)
    pieces = [None] * A_HEADS
    for kvh in range(A_KV_HEADS):
        for g in range(group):
            hq = group * kvh + g
            og = outs[kvh][g * tq:(g + 1) * tq]
            if hq % 2 != kvh % 2:
                og = pltpu.roll(og, HEAD_DIM, 1)
            pieces[hq] = og
    slabs = [jnp.where(lo_half, pieces[2 * s], pieces[2 * s + 1]) for s in range(A_HEADS // 2)]
    return jnp.concatenate(slabs, axis=1)


def _attn_project(x, g_ref, wa_ref):
    h = _rms(x, g_ref[...])
    proj = _mm(h, wa_ref[...])
    q = proj[:, :A_WIDTH] * ATTN_SCALE
    k = proj[:, A_WIDTH:A_WIDTH + KV_WIDTH]
    v = proj[:, A_WIDTH + KV_WIDTH:A_WIDTH + 2 * KV_WIDTH]
    ga = proj[:, A_WIDTH + 2 * KV_WIDTH:ATTN_COLS]
    ma = proj[:, ATTN_COLS:ATTN_COLS + D_MODEL]
    mb = proj[:, ATTN_COLS + D_MODEL:]
    return q, k, v, ga, ma, mb


def _merge_tail(x, o, ga, ma, mb, yb, woa_ref, wo_ref, fg_ref, final):
    ya = _mm(o * _silu(ga), woa_ref[...])
    merged = _sigmoid(ma) * ya + _sigmoid(mb) * yb
    out = x + _mm(merged, wo_ref[...])
    return _rms(out, fg_ref[...]) if final else out


def _attn_prompt_kernel(x_ref, ye_ref, yo_ref, g_ref, wa_ref, woa_ref, wo_ref, fg_ref, relb_ref, sinks_ref,
                        out_ref, kw_ref, vw_ref, kprev, vprev, bprev, bcur, *, final):
    b = pl.program_id(0)
    n = pl.program_id(1)

    @pl.when((b == 0) & (n == 0))
    def _():
        _init_bias(relb_ref, bprev, bcur)

    @pl.when(n == 0)
    def _():
        kprev[...] = jnp.zeros_like(kprev)
        vprev[...] = jnp.zeros_like(vprev)

    x = x_ref[0]
    q, k, v, ga, ma, mb = _attn_project(x, g_ref, wa_ref)
    kp, vp = kprev[...], vprev[...]
    os = []
    for j in range(ATTN_BLOCKS):
        rows = slice(WINDOW * j, WINDOW * (j + 1))
        os.append(_attn_core(q[rows], kp, vp, k[rows], v[rows], bprev, bcur, sinks_ref, WINDOW,
                             (n == 0) if j == 0 else None))
        kp, vp = k[rows], v[rows]
    kprev[...] = kp
    vprev[...] = vp
    kw_ref[0] = kp
    vw_ref[0] = vp
    o = jnp.concatenate(os, axis=0)
    yb = jnp.concatenate([(ye_ref if j % 2 == 0 else yo_ref)[0, j // 2] for j in range(ATTN_BLOCKS)], axis=0)
    out_ref[0] = _merge_tail(x, o, ga, ma, mb, yb, woa_ref, wo_ref, fg_ref, final)


def _attn_sample_kernel(x_ref, yb_ref, g_ref, wa_ref, woa_ref, wo_ref, fg_ref, relb_ref, sinks_ref, kc_ref, vc_ref,
                        out_ref, ko_ref, vo_ref, qbuf, kbuf, vbuf, obuf, bprev, bcur, *, seq, final):
    @pl.when(pl.program_id(0) == 0)
    def _():
        _init_bias(relb_ref, bprev, bcur)

    x = x_ref[...]
    q, k, v, ga, ma, mb = _attn_project(x, g_ref, wa_ref)
    qbuf[...] = q
    kbuf[...] = k
    vbuf[...] = v
    pad = jnp.zeros((WINDOW - seq, KV_WIDTH), F32)

    def body(j, carry):
        rows = pl.ds(pl.multiple_of(j * seq, seq), seq)
        kn = kbuf[rows, :]
        vn = vbuf[rows, :]
        kcj = kc_ref[j]
        vcj = vc_ref[j]
        o = _attn_core(qbuf[rows, :], kcj, vcj, jnp.concatenate([kn, pad], axis=0),
                       jnp.concatenate([vn, pad], axis=0), bprev, bcur, sinks_ref, seq, None)
        obuf[rows, :] = o
        ko_ref[j, pl.ds(0, WINDOW - seq), :] = kcj[seq:, :]
        ko_ref[j, pl.ds(WINDOW - seq, seq), :] = kn
        vo_ref[j, pl.ds(0, WINDOW - seq), :] = vcj[seq:, :]
        vo_ref[j, pl.ds(WINDOW - seq, seq), :] = vn
        return carry

    lax.fori_loop(0, ROWS // seq, body, 0, unroll=SEQ_UNROLL)
    out_ref[...] = _merge_tail(x, obuf[...], ga, ma, mb, yb_ref[...], woa_ref, wo_ref, fg_ref, final)


def _head_sums(x, ones_blocks, terms=2):
    outs = []
    width = ones_blocks.shape[0]
    for i in range(x.shape[1] // width):
        xs = x[:, width * i:width * (i + 1)]
        hi = _bf(xs)
        acc = jnp.dot(hi, ones_blocks, preferred_element_type=F32)
        if terms == 2:
            acc = acc + jnp.dot(_bf(xs - hi.astype(F32)), ones_blocks, preferred_element_type=F32)
        outs.append(acc)
    return jnp.concatenate(outs, axis=1)


def _seg_cumsum(x, seq):
    pos = lax.broadcasted_iota(jnp.int32, (x.shape[0], 1), 0) & (seq - 1)
    s = 1
    while s < seq:
        if s % 8 == 0 and seq == x.shape[0]:
            x = jnp.concatenate([x[:s], x[s:] + x[:-s]], axis=0)
        else:
            x = x + jnp.where(pos >= s, pltpu.roll(x, s, 0), 0.0)
        s *= 2
    return x


def _neumann_levels(ps, seq, out):
    n = ps[0].shape[0]
    ri = lax.broadcasted_iota(jnp.int32, (n, n), 0)
    ci = lax.broadcasted_iota(jnp.int32, (n, n), 1)
    eye = jnp.where(ri == ci, 1.0, 0.0)
    xs = [eye + p for p in ps]
    if seq > 2:
        pws = [_mm(p, p) for p in ps]
        yield
        span = 2
        while span < seq:
            last = 2 * span >= seq
            skip = span if span >= 16 else 0
            nxt_x, nxt_p = [], []
            for pw, x in zip(pws, xs):
                rhs = x if last else jnp.concatenate([pw, x], axis=1)
                res = _mm(pw[skip:], rhs)
                dx = res if last else res[:, n:]
                if skip:
                    nxt_x.append(jnp.concatenate([x[:skip], x[skip:] + dx], axis=0))
                else:
                    nxt_x.append(x + dx)
                if not last:
                    pn = res[:, :n]
                    nxt_p.append(jnp.concatenate([jnp.zeros((skip, n), F32), pn], axis=0) if skip else pn)
            xs, pws = nxt_x, nxt_p
            span *= 2
            yield
    out[:] = xs


PREP_NAMES = ("at", "rt", "bkt", "v", "gt", "bonus", "sgb", "bkh")
RWKV_WEIGHT_NAMES = ("g", "wr", "wob", "mu", "w0", "lora", "a0", "kk", "ka", "rk", "lng", "lnb")
PREP_SLAB = 2 * LANES


def _ones_blocks():
    ri = lax.broadcasted_iota(jnp.int32, (2 * LANES, 2 * LANES), 0)
    ci = lax.broadcasted_iota(jnp.int32, (2 * LANES, 2 * LANES), 1)
    shift = HEAD_DIM.bit_length() - 1
    return jnp.where((ri >> shift) == (ci >> shift), 1.0, 0.0).astype(BF16)


def _rwkv_prep(x, first_ref, w, seq, out, ps_ref, lora_ref):
    nseq = ROWS // seq
    h = _rms(x, w["g"][...])
    proj = _mm(h, w["wr"][...])
    ps_ref[...] = proj[:, :SHIFT_W]
    out["sgb"][...] = _silu(proj[:, SHIFT_W:])
    pos = lax.broadcasted_iota(jnp.int32, (ROWS, 1), 0) & (seq - 1)

    def shifted(cols):
        cur = ps_ref[:, cols]
        prev = jnp.where(pos == 0, first_ref[:, cols], pltpu.roll(cur, 1, 0))
        return cur + (prev - cur) * w["mu"][:, cols]

    zl = shifted(slice(3 * B_WIDTH, SHIFT_W))
    lo_half = lax.broadcasted_iota(jnp.int32, (1, LANES), 1) < HEAD_DIM
    lora_ref[...] = _mm(jnp.where(lo_half, jnp.tanh(zl), zl), w["lora"][...])
    yield
    ones_blocks = _ones_blocks()
    for d in range(B_WIDTH // PREP_SLAB):
        cols = slice(PREP_SLAB * d, PREP_SLAB * (d + 1))

        def wcols(name):
            return w[name][:, cols]

        r = shifted(cols)
        k = shifted(slice(B_WIDTH + cols.start, B_WIDTH + cols.stop))
        v = shifted(slice(2 * B_WIDTH + cols.start, 2 * B_WIDTH + cols.stop))
        lw = -math.exp(-0.5) * _sigmoid(wcols("w0") + lora_ref[:, cols])
        asig = _sigmoid(wcols("a0") + lora_ref[:, slice(B_WIDTH + cols.start, B_WIDTH + cols.stop)])
        kkr = k * wcols("kk")
        kk = kkr * lax.rsqrt(jnp.maximum(_head_sums(kkr * kkr, ones_blocks), 1e-24))
        kp = k * (1.0 + (asig - 1.0) * wcols("ka"))
        a_ = -kk
        b_ = kk * asig
        cum = _seg_cumsum(lw, seq)
        if nseq == 1:
            tot = jnp.broadcast_to(cum[ROWS - 1:ROWS, :], cum.shape)
        else:
            c3 = cum.reshape(nseq, seq, PREP_SLAB)
            tot = jnp.broadcast_to(c3[:, seq - 1:seq, :], c3.shape).reshape(cum.shape)
        g_inv = jnp.exp(-cum)
        g_last = jnp.exp(tot - cum)
        out["at"][:, cols] = a_ * jnp.exp(cum - lw)
        out["rt"][:, cols] = r * jnp.exp(cum)
        bt = b_ * g_inv
        kt = kp * g_inv
        bh = b_ * g_last
        kh = kp * g_last
        for q in range(PREP_SLAB // LANES):
            ql = slice(LANES * q, LANES * (q + 1))
            pair = PREP_SLAB // LANES * d + q
            out["bkt"][pair] = _bf(jnp.concatenate([bt[:, ql], kt[:, ql]], axis=0).T)
            out["bkh"][pair] = _bf(jnp.concatenate([bh[:, ql], kh[:, ql]], axis=0))
        out["gt"][:, cols] = jnp.exp(tot)
        out["v"][:, cols] = v
        out["bonus"][:, cols] = _head_sums(r * kp * wcols("rk"), ones_blocks) * v
        yield


def _rwkv_core(buf, w, seq, av_ref, t_ref, arbk_ref, res, *, s_ref=None, si_ref=None, so_ref=None,
               x0_ref=None, y0_ref=None, uvt_ref=None):
    nseq = ROWS // seq
    prompt = s_ref is not None
    lo_half = lax.broadcasted_iota(jnp.int32, (1, LANES), 1) < HEAD_DIM
    hi_half = jnp.logical_not(lo_half)
    rr = lax.broadcasted_iota(jnp.int32, (ROWS, ROWS), 0)
    cc = lax.broadcasted_iota(jnp.int32, (ROWS, ROWS), 1)
    shift = int(math.log2(seq))
    same = (rr >> shift) == (cc >> shift)
    strict = same & (rr > cc)
    incl = same & (rr >= cc)
    bd = (rr < HEAD_DIM) == (cc < HEAD_DIM)
    pair_slices = [slice(LANES * p, LANES * (p + 1)) for p in range(N_PAIRS)]
    at_ref, rt_ref, bkt_ref, v_ref, gt_ref, bkh_ref = (buf[n] for n in ("at", "rt", "bkt", "v", "gt", "bkh"))

    def halves(t):
        return jnp.concatenate([jnp.where(lo_half, t, 0.0), jnp.where(hi_half, t, 0.0)], axis=0)

    def blockdiag_rows(xa, xb):
        return jnp.concatenate([jnp.where(lo_half, xa, 0.0), jnp.where(lo_half, pltpu.roll(xa, HEAD_DIM, 1), 0.0),
                                jnp.where(hi_half, pltpu.roll(xb, HEAD_DIM, 1), 0.0), jnp.where(hi_half, xb, 0.0)],
                               axis=0)

    if prompt:
        s0s = [s_ref[p] for p in range(N_PAIRS)]
        s0ts = [_bf(s0.T) for s0 in s0s]
        x0s = [_mm(at_ref[:, sl], s0t) for sl, s0t in zip(pair_slices, s0ts)]
        y0s = [_mm(rt_ref[:, sl], s0t) for sl, s0t in zip(pair_slices, s0ts)]
    else:
        def read_body(j, carry):
            rows = pl.ds(pl.multiple_of(j * seq, seq), seq)
            for p, sl in enumerate(pair_slices):
                s0 = blockdiag_rows(si_ref[j, 2 * p], si_ref[j, 2 * p + 1])
                arj = jnp.concatenate([at_ref[rows, sl], rt_ref[rows, sl]], axis=0)
                xy = _mm_nt(arj, s0)
                x0_ref[rows, sl] = xy[:seq]
                y0_ref[rows, sl] = xy[seq:]
            return carry
        lax.fori_loop(0, nseq, read_body, 0, unroll=SEQ_UNROLL)
        x0s = [x0_ref[:, sl] for sl in pair_slices]
        y0s = [y0_ref[:, sl] for sl in pair_slices]
    yield

    a_abs, a_aks = [], []
    for p, sl in enumerate(pair_slices):
        rbk = []
        for hh in range(2):
            half = lo_half if hh == 0 else hi_half
            ga = _mm(jnp.where(half, at_ref[:, sl], 0.0), bkt_ref[p])
            gr = _mm(jnp.where(half, rt_ref[:, sl], 0.0), bkt_ref[p])
            a_abs.append(jnp.where(strict, ga[:, :ROWS], 0.0))
            a_aks.append(jnp.where(strict, ga[:, ROWS:], 0.0))
            rbk.append(jnp.where(incl, gr[:, :ROWS], 0.0))
            rbk.append(jnp.where(incl, gr[:, ROWS:], 0.0))
        arbk_ref[p] = _bf(jnp.concatenate(rbk, axis=1))
        av_ref[:, sl] = _mm(jnp.concatenate(a_aks[2 * p:2 * p + 2], axis=1), halves(v_ref[:, sl]))
        if p % 4 == 3:
            yield
    ts = []
    yield from _neumann_levels(a_abs, seq, ts)
    for p in range(N_PAIRS):
        t_ref[p] = _bf(jnp.concatenate(ts[2 * p:2 * p + 2], axis=1))

    vss = [v_ref[:, sl] for sl in pair_slices]
    us = [jnp.dot(t_ref[p], _bf(halves(x0s[p] + av_ref[:, sl])), preferred_element_type=F32)
          for p, sl in enumerate(pair_slices)]
    yield
    ys = []
    for p in range(N_PAIRS):
        u, vs = us[p], vss[p]
        rhs = jnp.concatenate([jnp.where(lo_half, u, 0.0), jnp.where(lo_half, vs, 0.0),
                               jnp.where(hi_half, u, 0.0), jnp.where(hi_half, vs, 0.0)], axis=0)
        ys.append(y0s[p] + jnp.dot(arbk_ref[p], _bf(rhs), preferred_element_type=F32))
    yield

    uvts = [jnp.concatenate([us[p], vss[p]], axis=0).T for p in range(N_PAIRS)]
    if prompt:
        s1s = [s0s[p] * gt_ref[0:1, sl] + jnp.where(bd, _mm(uvts[p], bkh_ref[p]), 0.0)
               for p, sl in enumerate(pair_slices)]
        for p in range(N_PAIRS):
            s_ref[p] = s1s[p]
        res["s1"] = s1s
    else:
        colseq = (lax.broadcasted_iota(jnp.int32, (1, 2 * ROWS), 1) & (ROWS - 1)) >> shift
        for p in range(N_PAIRS):
            uvt_ref[p] = _bf(uvts[p])

        def upd_body(j, carry):
            rows = pl.ds(pl.multiple_of(j * seq, seq), seq)
            q = HEAD_DIM // 2
            for p, sl in enumerate(pair_slices):
                uvt = jnp.where(colseq == j, uvt_ref[p], jnp.zeros((), BF16))
                upd = jnp.dot(uvt, bkh_ref[p], preferred_element_type=F32)
                gt = gt_ref[rows, sl]
                gsw = pltpu.roll(gt, HEAD_DIM, 1)
                so_ref[j, 2 * p] = (si_ref[j, 2 * p] * jnp.where(lo_half, gt, gsw)[0:1]
                                    + jnp.where(lo_half, upd[0:q], pltpu.roll(upd[q:2 * q], HEAD_DIM, 1)))
                so_ref[j, 2 * p + 1] = (si_ref[j, 2 * p + 1] * jnp.where(lo_half, gsw, gt)[0:1]
                                        + jnp.where(lo_half, pltpu.roll(upd[2 * q:3 * q], HEAD_DIM, 1), upd[3 * q:]))
            return carry
        lax.fori_loop(0, nseq, upd_body, 0, unroll=SEQ_UNROLL)
    yield

    ones_blocks = _ones_blocks()
    y = jnp.concatenate(ys, axis=1)
    mean = _head_sums(y, ones_blocks, terms=1) * (1.0 / HEAD_DIM)
    dlt = y - mean
    var = _head_sums(dlt * dlt, ones_blocks, terms=1) * (1.0 / HEAD_DIM)
    yn = dlt * lax.rsqrt(var + GN_EPS) * w["lng"][...] + w["lnb"][...]
    yo = (yn + buf["bonus"][...]) * buf["sgb"][...]
    res["yb"] = _mm(yo, w["wob"][...])


N_RWKV_W = len(RWKV_WEIGHT_NAMES)
N_PREP = len(PREP_NAMES)
PROMPT_ORDER = "CCPCCPCCPCCPCCPCCC"


def _rwkv_prompt_kernel(*refs, tiles_per_seq, n_tiles):
    x_ref = refs[0]
    w = dict(zip(RWKV_WEIGHT_NAMES, refs[1:1 + N_RWKV_W]))
    ye_ref, yo_ref, so_ref, sh_ref = refs[1 + N_RWKV_W:5 + N_RWKV_W]
    scratch = refs[5 + N_RWKV_W:]
    set_a = dict(zip(PREP_NAMES, scratch[:N_PREP]))
    set_b = dict(zip(PREP_NAMES, scratch[N_PREP:2 * N_PREP]))
    av_ref, lora_ref, t_ref, arbk_ref, s_ref, ps_ref, carry_ref = scratch[2 * N_PREP:]
    j = pl.program_id(0)
    starts_seq = lax.rem(2 * j, tiles_per_seq) == 0

    @pl.when(j == 0)
    def _():
        for name in PREP_NAMES:
            set_b[name][...] = jnp.zeros_like(set_b[name])
        s_ref[...] = jnp.zeros_like(s_ref)

    @pl.when(starts_seq)
    def _():
        carry_ref[...] = jnp.zeros_like(carry_ref)

    def half_step(x, prepared, current):
        res = {}
        prep = _rwkv_prep(x, carry_ref, w, ROWS, prepared, ps_ref, lora_ref)
        core = _rwkv_core(current, w, ROWS, av_ref, t_ref, arbk_ref, res, s_ref=s_ref)
        for kind in PROMPT_ORDER:
            next(prep if kind == "P" else core, None)
        for _ in prep:
            pass
        for _ in core:
            pass
        carry_ref[...] = ps_ref[ROWS - 1:ROWS, :]
        return res

    res = half_step(x_ref[0, 0:ROWS, :], set_a, set_b)
    yo_ref[0, 0] = res["yb"]

    @pl.when(starts_seq & (j > 0))
    def _():
        lo_half = lax.broadcasted_iota(jnp.int32, (1, LANES), 1) < HEAD_DIM
        for p in range(N_PAIRS):
            so_ref[0, p] = jnp.where(lo_half, res["s1"][p][:HEAD_DIM], res["s1"][p][HEAD_DIM:])

    @pl.when(starts_seq)
    def _():
        s_ref[...] = jnp.zeros_like(s_ref)

    res = half_step(x_ref[0, ROWS:2 * ROWS, :], set_b, set_a)
    sh_ref[0] = ps_ref[ROWS - 1:ROWS, :]

    @pl.when(2 * j < n_tiles)
    def _():
        ye_ref[0, 0] = res["yb"]


def _rwkv_sample_kernel(*refs, seq):
    x_ref = refs[0]
    w = dict(zip(RWKV_WEIGHT_NAMES, refs[1:1 + N_RWKV_W]))
    si_ref, shin_ref, yb_ref, so_ref, ps_ref = refs[1 + N_RWKV_W:6 + N_RWKV_W]
    scratch = refs[6 + N_RWKV_W:]
    buf = dict(zip(PREP_NAMES, scratch[:N_PREP]))
    av_ref, lora_ref, t_ref, arbk_ref, x0_ref, y0_ref, uvt_ref = scratch[N_PREP:]
    res = {}
    for _ in _rwkv_prep(x_ref[...], shin_ref, w, seq, buf, ps_ref, lora_ref):
        pass
    for _ in _rwkv_core(buf, w, seq, av_ref, t_ref, arbk_ref, res, si_ref=si_ref, so_ref=so_ref,
                        x0_ref=x0_ref, y0_ref=y0_ref, uvt_ref=uvt_ref):
        pass
    yb_ref[...] = res["yb"]


def _const_spec(shape):
    nd = len(shape)
    return pl.BlockSpec(shape, lambda *_: (0,) * nd, pipeline_mode=pl.Buffered(1))


def _smem_spec():
    return pl.BlockSpec(memory_space=pltpu.SMEM)


def _params(n_axes):
    return pltpu.CompilerParams(dimension_semantics=("arbitrary",) * n_axes, vmem_limit_bytes=VMEM_LIMIT)


def _attn_weight_specs():
    return [_const_spec((1, D_MODEL)), _const_spec((D_MODEL, ATTN_COLS + MERGE_COLS)),
            _const_spec((A_WIDTH, D_MODEL)), _const_spec((D_MODEL, D_MODEL)), _const_spec((1, D_MODEL)),
            _smem_spec(), _smem_spec()]


def _attention_prompt(x, yb_even, yb_odd, weights, final):
    bsz, t, _ = x.shape
    rows = ATTN_BLOCKS * WINDOW
    nb = t // rows
    tile = pl.BlockSpec((1, rows, D_MODEL), lambda b, n: (b, n, 0))
    half = pl.BlockSpec((1, ATTN_BLOCKS // 2, ROWS, D_MODEL), lambda b, n: (b, n, 0, 0))
    win = pl.BlockSpec((1, WINDOW, KV_WIDTH), lambda b, n: (b, 0, 0))
    return pl.pallas_call(
        functools.partial(_attn_prompt_kernel, final=final),
        grid=(bsz, nb),
        in_specs=[tile, half, half] + _attn_weight_specs(),
        out_specs=[tile, win, win],
        out_shape=[jax.ShapeDtypeStruct((bsz, t, D_MODEL), F32),
                   jax.ShapeDtypeStruct((bsz, WINDOW, KV_WIDTH), F32),
                   jax.ShapeDtypeStruct((bsz, WINDOW, KV_WIDTH), F32)],
        scratch_shapes=[pltpu.VMEM((WINDOW, KV_WIDTH), F32), pltpu.VMEM((WINDOW, KV_WIDTH), F32),
                        pltpu.VMEM((A_HEADS, WINDOW, WINDOW), F32), pltpu.VMEM((A_HEADS, WINDOW, WINDOW), F32)],
        compiler_params=_params(2),
        name="attn_prompt",
    )(x, yb_even, yb_odd, *weights)


def _attention_sample(x2d, yb, seq, weights, kc, vc, final):
    rows = x2d.shape[0]
    nseq = ROWS // seq
    tile = pl.BlockSpec((ROWS, D_MODEL), lambda i: (i, 0))
    cache = pl.BlockSpec((nseq, WINDOW, KV_WIDTH), lambda i: (i, 0, 0))
    return pl.pallas_call(
        functools.partial(_attn_sample_kernel, seq=seq, final=final),
        grid=(rows // ROWS,),
        in_specs=[tile, tile] + _attn_weight_specs() + [cache, cache],
        out_specs=[tile, cache, cache],
        out_shape=[jax.ShapeDtypeStruct((rows, D_MODEL), F32),
                   jax.ShapeDtypeStruct(kc.shape, F32), jax.ShapeDtypeStruct(vc.shape, F32)],
        scratch_shapes=[pltpu.VMEM((ROWS, A_WIDTH), F32), pltpu.VMEM((ROWS, KV_WIDTH), F32),
                        pltpu.VMEM((ROWS, KV_WIDTH), F32), pltpu.VMEM((ROWS, A_WIDTH), F32),
                        pltpu.VMEM((A_HEADS, WINDOW, WINDOW), F32), pltpu.VMEM((A_HEADS, WINDOW, WINDOW), F32)],
        compiler_params=_params(1),
        name="attn_sample",
    )(x2d, yb, *weights, kc, vc)


def _prep_scratch(names):
    shapes = {"bkt": pltpu.VMEM((N_PAIRS, LANES, 2 * ROWS), BF16), "bkh": pltpu.VMEM((N_PAIRS, 2 * ROWS, LANES), BF16)}
    return [shapes.get(n, pltpu.VMEM((ROWS, B_WIDTH), F32)) for n in names]


def _core_scratch():
    return [pltpu.VMEM((ROWS, B_WIDTH), F32), pltpu.VMEM((ROWS, 2 * B_WIDTH), F32),
            pltpu.VMEM((N_PAIRS, ROWS, 2 * ROWS), BF16), pltpu.VMEM((N_PAIRS, ROWS, 4 * ROWS), BF16)]


def _rwkv_weight_specs():
    row = _const_spec((1, B_WIDTH))
    return [_const_spec((1, D_MODEL)), _const_spec((D_MODEL, RWKV_COLS)), _const_spec((B_WIDTH, D_MODEL)),
            _const_spec((1, SHIFT_W)), row, _const_spec((LANES, 2 * B_WIDTH)), row, row, row, row, row, row]


def _rwkv_prompt(x, weights):
    bsz, t, _ = x.shape
    nb = t // ROWS
    npair = nb // 2
    total = bsz * npair

    def pair_of(jj):
        return lax.div(jj, npair), lax.rem(jj, npair)

    def x_map(j):
        return (*pair_of(jnp.minimum(j, total - 1)), 0)

    def even_map(j):
        return (*pair_of(jnp.minimum(j, total - 1)), 0, 0)

    def odd_map(j):
        return (*pair_of(jnp.maximum(j - 1, 0)), 0, 0)

    ytile = jax.ShapeDtypeStruct((bsz, npair, ROWS, D_MODEL), F32)
    return pl.pallas_call(
        functools.partial(_rwkv_prompt_kernel, tiles_per_seq=nb, n_tiles=bsz * nb),
        grid=(total + 1,),
        in_specs=[pl.BlockSpec((1, 2 * ROWS, D_MODEL), x_map)] + _rwkv_weight_specs(),
        out_specs=[pl.BlockSpec((1, 1, ROWS, D_MODEL), even_map),
                   pl.BlockSpec((1, 1, ROWS, D_MODEL), odd_map),
                   pl.BlockSpec((1, N_PAIRS, HEAD_DIM, LANES), lambda j: (odd_map(j)[0], 0, 0, 0)),
                   pl.BlockSpec((1, 1, SHIFT_W), lambda j: (x_map(j)[0], 0, 0))],
        out_shape=[ytile, ytile,
                   jax.ShapeDtypeStruct((bsz, N_PAIRS, HEAD_DIM, LANES), F32),
                   jax.ShapeDtypeStruct((bsz, 1, SHIFT_W), F32)],
        scratch_shapes=_prep_scratch(PREP_NAMES) + _prep_scratch(PREP_NAMES) + _core_scratch()
        + [pltpu.VMEM((N_PAIRS, LANES, LANES), F32), pltpu.VMEM((ROWS, SHIFT_W), F32),
           pltpu.VMEM((1, SHIFT_W), F32)],
        compiler_params=_params(1),
        name="rwkv_prompt",
    )(x, *weights)


def _rwkv_sample(x2d, seq, weights, s_nat, shift_rows):
    rows = x2d.shape[0]
    nseq = ROWS // seq
    tile = pl.BlockSpec((ROWS, D_MODEL), lambda i: (i, 0))
    state = pl.BlockSpec((nseq, B_HEADS, HEAD_DIM // 2, LANES), lambda i: (i, 0, 0, 0))
    shift = pl.BlockSpec((ROWS, SHIFT_W), lambda i: (i, 0))
    wide = pltpu.VMEM((ROWS, B_WIDTH), F32)
    return pl.pallas_call(
        functools.partial(_rwkv_sample_kernel, seq=seq),
        grid=(rows // ROWS,),
        in_specs=[tile] + _rwkv_weight_specs() + [state, shift],
        out_specs=[tile, state, shift],
        out_shape=[jax.ShapeDtypeStruct((rows, D_MODEL), F32),
                   jax.ShapeDtypeStruct(s_nat.shape, F32),
                   jax.ShapeDtypeStruct((rows, SHIFT_W), F32)],
        scratch_shapes=_prep_scratch(PREP_NAMES) + _core_scratch()
        + [wide, wide, pltpu.VMEM((N_PAIRS, ROWS, 2 * ROWS), BF16)],
        compiler_params=_params(1),
        name="rwkv_sample",
    )(x2d, *weights, s_nat, shift_rows)


def _even_odd(a, axis):
    a = jnp.moveaxis(a, axis, -1)
    shape = a.shape
    a = a.reshape(shape[:-1] + (B_HEADS, HEAD_DIM // 2, 2)).swapaxes(-1, -2).reshape(shape)
    return jnp.moveaxis(a, -1, axis)


def _natural(a, axis):
    a = jnp.moveaxis(a, axis, -1)
    shape = a.shape
    a = a.reshape(shape[:-1] + (B_HEADS, 2, HEAD_DIM // 2)).swapaxes(-1, -2).reshape(shape)
    return jnp.moveaxis(a, -1, axis)


def _shift_even_odd(a, fn):
    return jnp.concatenate([a[..., :2 * B_WIDTH], fn(a[..., 2 * B_WIDTH:3 * B_WIDTH], -1), a[..., 3 * B_WIDTH:]],
                           axis=-1)


def _heads_from_pairs(s):
    b = s.shape[0]
    s = s.reshape(b, N_PAIRS, 2, HEAD_DIM // 2, 2, HEAD_DIM)
    return s.transpose(0, 1, 4, 3, 2, 5).reshape(b, B_HEADS, HEAD_DIM, HEAD_DIM)


def kernel(x_prompt, x_sample, cache_k_win, cache_v_win, state_wkv, state_shift, rel_bias, norm_g, w_in, attn_sinks, shift_mu, rwkv_w0, rwkv_w2, rwkv_a0, rwkv_a2, rwkv_k_k, rwkv_k_a, rwkv_r_k, lnx_g, lnx_b, w_out_a, w_out_b, w_o, final_g):
    depth = w_in.shape[0]
    bsz, t, _ = x_prompt.shape
    dbsz, dseq, _ = x_sample.shape
    hp = x_prompt
    hs = x_sample.reshape(dbsz * dseq, D_MODEL)
    fg = final_g.reshape(1, D_MODEL)
    outs = [[] for _ in range(8)]
    for l in range(depth):
        g = norm_g[l].reshape(1, D_MODEL)
        w = w_in[l]
        wa = _bf(jnp.concatenate([w[:, :ATTN_COLS], w[:, ATTN_COLS + RWKV_COLS:]], axis=1))
        wr = w[:, ATTN_COLS:ATTN_COLS + RWKV_COLS]
        wr = _bf(jnp.concatenate([_shift_even_odd(wr[:, :SHIFT_W], _even_odd), _even_odd(wr[:, SHIFT_W:], -1)],
                                 axis=1))
        woa, wob, wo = _bf(w_out_a[l]), _bf(_even_odd(w_out_b[l], 0)), _bf(w_o[l])
        zeros = jnp.zeros((DECAY_LORA, B_WIDTH), F32)
        lora = _bf(jnp.concatenate([jnp.concatenate([rwkv_w2[l], zeros], axis=1),
                                    jnp.concatenate([zeros, rwkv_a2[l]], axis=1)], axis=0))
        rw = [g, wr, wob, _shift_even_odd(shift_mu[l].reshape(1, SHIFT_W), _even_odd),
              rwkv_w0[l].reshape(1, B_WIDTH), lora,
              rwkv_a0[l].reshape(1, B_WIDTH), rwkv_k_k[l].reshape(1, B_WIDTH), rwkv_k_a[l].reshape(1, B_WIDTH),
              rwkv_r_k[l].reshape(1, B_WIDTH), _even_odd(lnx_g[l].reshape(1, B_WIDTH), -1),
              _even_odd(lnx_b[l].reshape(1, B_WIDTH), -1)]
        aw = [g, wa, woa, wo, fg, rel_bias, attn_sinks[l]]
        final = l == depth - 1

        yb_even, yb_odd, s1, t1 = _rwkv_prompt(hp, rw)
        hp, k1, v1 = _attention_prompt(hp, yb_even, yb_odd, aw, final)

        kc = cache_k_win[l].reshape(dbsz, WINDOW, KV_WIDTH)
        vc = cache_v_win[l].reshape(dbsz, WINDOW, KV_WIDTH)
        shift_rows = jnp.repeat(_shift_even_odd(state_shift[l], _even_odd), dseq, axis=0)
        s_rows = state_wkv[l].reshape(dbsz, B_HEADS, HEAD_DIM // 2, LANES)
        yb_s, s2, ps_s = _rwkv_sample(hs, dseq, rw, s_rows, shift_rows)
        hs, k2, v2 = _attention_sample(hs, yb_s, dseq, aw, kc, vc, final)

        outs[0].append(k1.reshape(bsz, WINDOW, A_KV_HEADS, HEAD_DIM))
        outs[1].append(v1.reshape(bsz, WINDOW, A_KV_HEADS, HEAD_DIM))
        outs[2].append(_heads_from_pairs(s1))
        outs[3].append(_shift_even_odd(t1.reshape(bsz, SHIFT_W), _natural))
        outs[4].append(k2.reshape(dbsz, WINDOW, A_KV_HEADS, HEAD_DIM))
        outs[5].append(v2.reshape(dbsz, WINDOW, A_KV_HEADS, HEAD_DIM))
        outs[6].append(s2.reshape(dbsz, B_HEADS, HEAD_DIM, HEAD_DIM))
        outs[7].append(_shift_even_odd(ps_s.reshape(dbsz, dseq, SHIFT_W)[:, -1], _natural))
    y_prompt = hp
    y_sample = hs.reshape(dbsz, dseq, D_MODEL)
    return (y_prompt, y_sample) + tuple(jnp.stack(o) for o in outs)
```

```python
import functools
import math

import numpy as np
import jax
import jax.numpy as jnp
from jax import lax
from jax.experimental import pallas as pl
from jax.experimental.pallas import tpu as pltpu

D_MODEL = 1024
HEAD_DIM = 64
A_HEADS = 16
A_KV_HEADS = 4
A_WIDTH = A_HEADS * HEAD_DIM
KV_WIDTH = A_KV_HEADS * HEAD_DIM
WINDOW = 128
ATTN_SCALE = HEAD_DIM ** -0.5
N_BUCKETS = 32
MAX_DISTANCE = 128
B_HEADS = 16
B_WIDTH = B_HEADS * HEAD_DIM
DECAY_LORA = 64
A_LORA = 64
SHIFT_W = 3 * B_WIDTH + DECAY_LORA + A_LORA
GN_EPS = 64e-5
NORM_EPS = 1e-6

ATTN_COLS = 2 * A_WIDTH + 2 * KV_WIDTH
RWKV_COLS = SHIFT_W + B_WIDTH
MERGE_COLS = 2 * D_MODEL

ROWS = 128
LANES = 128
N_PAIRS = B_HEADS // 2
ATTN_BLOCKS = 4
SEQ_UNROLL = 4
NEG = -1e30
VMEM_LIMIT = 56 * 1024 * 1024

F32 = jnp.float32
BF16 = jnp.bfloat16


def _bucket_ranges():
    d = np.arange(0, WINDOW + 1)
    max_exact = N_BUCKETS // 2
    df = np.maximum(d, 1).astype(np.float32)
    large = max_exact + (np.log(df / np.float32(max_exact)) / np.float32(math.log(MAX_DISTANCE / max_exact))
                         * np.float32(N_BUCKETS - max_exact)).astype(np.int32)
    large = np.minimum(large, N_BUCKETS - 1)
    bucket = np.where(d < max_exact, d, large)
    out = []
    for b in range(N_BUCKETS):
        idx = np.nonzero(bucket == b)[0]
        if idx.size:
            assert idx[-1] - idx[0] + 1 == idx.size
            out.append((b, int(idx[0]), int(idx[-1])))
    return tuple(out)


_BUCKET_RANGES = _bucket_ranges()


def _bf(x):
    return x.astype(BF16)


def _mm(a, b):
    return jnp.dot(_bf(a), _bf(b), preferred_element_type=F32)


def _mm_nt(a, b):
    return lax.dot_general(_bf(a), _bf(b), (((1,), (1,)), ((), ())), preferred_element_type=F32)


def _rms(x, g):
    ms = jnp.mean(x * x, axis=-1, keepdims=True)
    return x * lax.rsqrt(ms + NORM_EPS) * g


def _sigmoid(x):
    return 1.0 / (1.0 + jnp.exp(-x))


def _silu(x):
    return x * _sigmoid(x)


def _init_bias(relb_ref, bprev_ref, bcur_ref):
    qi = lax.broadcasted_iota(jnp.int32, (WINDOW, WINDOW), 0)
    kj = lax.broadcasted_iota(jnp.int32, (WINDOW, WINDOW), 1)
    delta = qi - kj

    def body(h, carry):
        bp = jnp.full((WINDOW, WINDOW), NEG, F32)
        bc = jnp.full((WINDOW, WINDOW), NEG, F32)
        for (b, lo, hi) in _BUCKET_RANGES:
            val = relb_ref[b, h]
            bp = jnp.where((delta >= lo - WINDOW) & (delta <= hi - WINDOW), val, bp)
            bc = jnp.where((delta >= lo) & (delta <= hi), val, bc)
        bprev_ref[h] = bp
        bcur_ref[h] = bc
        return carry

    lax.fori_loop(0, A_HEADS, body, 0)


def _attn_core(q, kp, vp, kc, vc, bprev_ref, bcur_ref, sinks_ref, tq, first):
    group = A_HEADS // A_KV_HEADS
    lo_half = lax.broadcasted_iota(jnp.int32, (1, LANES), 1) < HEAD_DIM
    hi_half = jnp.logical_not(lo_half)
    rowi = lax.broadcasted_iota(jnp.int32, (group * tq, 1), 0)
    sps, scs, sinks = [], [], []
    for kvh in range(A_KV_HEADS):
        i, c = divmod(kvh, 2)
        sl = slice(LANES * i, LANES * (i + 1))
        rows = []
        for g in range(group):
            hq = group * kvh + g
            qs = q[:, LANES * (hq // 2):LANES * (hq // 2 + 1)]
            qm = jnp.where(lo_half if hq % 2 == 0 else hi_half, qs, 0.0)
            if hq % 2 != c:
                qm = pltpu.roll(qm, HEAD_DIM, 1)
            rows.append(qm)
        lhs = _bf(jnp.concatenate(rows, axis=0))
        bp = bprev_ref[group * kvh:group * (kvh + 1), 0:tq, :].reshape(group * tq, WINDOW)
        bc = bcur_ref[group * kvh:group * (kvh + 1), 0:tq, :].reshape(group * tq, WINDOW)
        sp = _mm_nt(lhs, kp[:, sl]) + bp
        if first is not None:
            sp = jnp.where(first, NEG, sp)
        sps.append(sp)
        scs.append(_mm_nt(lhs, kc[:, sl]) + bc)
        sink = sinks_ref[group * kvh + group - 1]
        for g in range(group - 2, -1, -1):
            sink = jnp.where(rowi < (g + 1) * tq, sinks_ref[group * kvh + g], sink)
        sinks.append(sink)
    ms = [jnp.maximum(jnp.max(jnp.maximum(sp, sc), axis=-1, keepdims=True), sink)
          for sp, sc, sink in zip(sps, scs, sinks)]
    pps = [jnp.exp(sp - m) for sp, m in zip(sps, ms)]
    pcs = [jnp.exp(sc - m) for sc, m in zip(scs, ms)]
    outs = []
    for kvh in range(A_KV_HEADS):
        i, c = divmod(kvh, 2)
        sl = slice(LANES * i, LANES * (i + 1))
        own = lo_half if c == 0 else hi_half
        o = _mm(pps[kvh], jnp.where(own, vp[:, sl], 1.0)) + _mm(pcs[kvh], jnp.where(own, vc[:, sl], 1.0))
        den = pltpu.roll(o, HEAD_DIM, 1) + jnp.exp(sinks[kvh] - ms[kvh])
        outs.append(o / den)
    pieces = [None] * A_HEADS
    for kvh in range(A_KV_HEADS):
        for g in range(group):
            hq = group * kvh + g
            og = outs[kvh][g * tq:(g + 1) * tq]
            if hq % 2 != kvh % 2:
                og = pltpu.roll(og, HEAD_DIM, 1)
            pieces[hq] = og
    slabs = [jnp.where(lo_half, pieces[2 * s], pieces[2 * s + 1]) for s in range(A_HEADS // 2)]
    return jnp.concatenate(slabs, axis=1)


def _attn_project(x, g_ref, wa_ref):
    h = _rms(x, g_ref[...])
    proj = _mm(h, wa_ref[...])
    q = proj[:, :A_WIDTH] * ATTN_SCALE
    k = proj[:, A_WIDTH:A_WIDTH + KV_WIDTH]
    v = proj[:, A_WIDTH + KV_WIDTH:A_WIDTH + 2 * KV_WIDTH]
    ga = proj[:, A_WIDTH + 2 * KV_WIDTH:ATTN_COLS]
    ma = proj[:, ATTN_COLS:ATTN_COLS + D_MODEL]
    mb = proj[:, ATTN_COLS + D_MODEL:]
    return q, k, v, ga, ma, mb


def _merge_tail(x, o, ga, ma, mb, yb, woa_ref, wo_ref, fg_ref, final):
    ya = _mm(o * _silu(ga), woa_ref[...])
    merged = _sigmoid(ma) * ya + _sigmoid(mb) * yb
    out = x + _mm(merged, wo_ref[...])
    return _rms(out, fg_ref[...]) if final else out


def _attn_prompt_kernel(x_ref, ye_ref, yo_ref, g_ref, wa_ref, woa_ref, wo_ref, fg_ref, relb_ref, sinks_ref,
                        out_ref, kw_ref, vw_ref, kprev, vprev, bprev, bcur, *, final):
    b = pl.program_id(0)
    n = pl.program_id(1)

    @pl.when((b == 0) & (n == 0))
    def _():
        _init_bias(relb_ref, bprev, bcur)

    @pl.when(n == 0)
    def _():
        kprev[...] = jnp.zeros_like(kprev)
        vprev[...] = jnp.zeros_like(vprev)

    x = x_ref[0]
    h = _bf(_rms(x, g_ref[...]))
    qkv_cols = A_WIDTH + 2 * KV_WIDTH
    qkv = jnp.dot(h, wa_ref[:, :qkv_cols], preferred_element_type=F32)
    q = qkv[:, :A_WIDTH] * ATTN_SCALE
    k = qkv[:, A_WIDTH:A_WIDTH + KV_WIDTH]
    v = qkv[:, A_WIDTH + KV_WIDTH:]
    rest = [slice(qkv_cols + D_MODEL * c, qkv_cols + D_MODEL * (c + 1)) for c in range(3)]
    assert ATTN_BLOCKS >= len(rest)
    gates = []
    kp, vp = kprev[...], vprev[...]
    os = []
    for j in range(ATTN_BLOCKS):
        rows = slice(WINDOW * j, WINDOW * (j + 1))
        os.append(_attn_core(q[rows], kp, vp, k[rows], v[rows], bprev, bcur, sinks_ref, WINDOW,
                             (n == 0) if j == 0 else None))
        kp, vp = k[rows], v[rows]
        if j < len(rest):
            gates.append(jnp.dot(h, wa_ref[:, rest[j]], preferred_element_type=F32))
    ga, ma, mb = gates
    kprev[...] = kp
    vprev[...] = vp
    kw_ref[0] = kp
    vw_ref[0] = vp
    o = jnp.concatenate(os, axis=0)
    yb = jnp.concatenate([(ye_ref if j % 2 == 0 else yo_ref)[0, j // 2] for j in range(ATTN_BLOCKS)], axis=0)
    out_ref[0] = _merge_tail(x, o, ga, ma, mb, yb, woa_ref, wo_ref, fg_ref, final)


def _attn_sample_kernel(x_ref, yb_ref, g_ref, wa_ref, woa_ref, wo_ref, fg_ref, relb_ref, sinks_ref, kc_ref, vc_ref,
                        out_ref, ko_ref, vo_ref, qbuf, kbuf, vbuf, obuf, bprev, bcur, *, seq, final):
    @pl.when(pl.program_id(0) == 0)
    def _():
        _init_bias(relb_ref, bprev, bcur)

    x = x_ref[...]
    q, k, v, ga, ma, mb = _attn_project(x, g_ref, wa_ref)
    qbuf[...] = q
    kbuf[...] = k
    vbuf[...] = v
    pad = jnp.zeros((WINDOW - seq, KV_WIDTH), F32)

    def body(j, carry):
        rows = pl.ds(pl.multiple_of(j * seq, seq), seq)
        kn = kbuf[rows, :]
        vn = vbuf[rows, :]
        kcj = kc_ref[j]
        vcj = vc_ref[j]
        o = _attn_core(qbuf[rows, :], kcj, vcj, jnp.concatenate([kn, pad], axis=0),
                       jnp.concatenate([vn, pad], axis=0), bprev, bcur, sinks_ref, seq, None)
        obuf[rows, :] = o
        ko_ref[j, pl.ds(0, WINDOW - seq), :] = kcj[seq:, :]
        ko_ref[j, pl.ds(WINDOW - seq, seq), :] = kn
        vo_ref[j, pl.ds(0, WINDOW - seq), :] = vcj[seq:, :]
        vo_ref[j, pl.ds(WINDOW - seq, seq), :] = vn
        return carry

    lax.fori_loop(0, ROWS // seq, body, 0, unroll=SEQ_UNROLL)
    out_ref[...] = _merge_tail(x, obuf[...], ga, ma, mb, yb_ref[...], woa_ref, wo_ref, fg_ref, final)


def _head_sums(x, ones_blocks, terms=2):
    outs = []
    width = ones_blocks.shape[0]
    for i in range(x.shape[1] // width):
        xs = x[:, width * i:width * (i + 1)]
        hi = _bf(xs)
        acc = jnp.dot(hi, ones_blocks, preferred_element_type=F32)
        if terms == 2:
            acc = acc + jnp.dot(_bf(xs - hi.astype(F32)), ones_blocks, preferred_element_type=F32)
        outs.append(acc)
    return jnp.concatenate(outs, axis=1)


def _seg_cumsum(x, seq):
    pos = lax.broadcasted_iota(jnp.int32, (x.shape[0], 1), 0) & (seq - 1)
    s = 1
    while s < seq:
        if s % 8 == 0 and seq == x.shape[0]:
            x = jnp.concatenate([x[:s], x[s:] + x[:-s]], axis=0)
        else:
            x = x + jnp.where(pos >= s, pltpu.roll(x, s, 0), 0.0)
        s *= 2
    return x


def _neumann_levels(ps, seq, out):
    n = ps[0].shape[0]
    ri = lax.broadcasted_iota(jnp.int32, (n, n), 0)
    ci = lax.broadcasted_iota(jnp.int32, (n, n), 1)
    eye = jnp.where(ri == ci, 1.0, 0.0)
    xs = [eye + p for p in ps]
    if seq > 2:
        pws = [_mm(p, p) for p in ps]
        yield
        span = 2
        while span < seq:
            last = 2 * span >= seq
            skip = span if span >= 16 else 0
            nxt_x, nxt_p = [], []
            for pw, x in zip(pws, xs):
                rhs = x if last else jnp.concatenate([pw, x], axis=1)
                res = _mm(pw[skip:], rhs)
                dx = res if last else res[:, n:]
                if skip:
                    nxt_x.append(jnp.concatenate([x[:skip], x[skip:] + dx], axis=0))
                else:
                    nxt_x.append(x + dx)
                if not last:
                    pn = res[:, :n]
                    nxt_p.append(jnp.concatenate([jnp.zeros((skip, n), F32), pn], axis=0) if skip else pn)
            xs, pws = nxt_x, nxt_p
            span *= 2
            yield
    out[:] = xs


PREP_NAMES = ("at", "rt", "bkt", "v", "gt", "bonus", "sgb", "bkh")
RWKV_WEIGHT_NAMES = ("g", "wr", "wob", "mu", "w0", "lora", "a0", "kk", "ka", "rk", "lng", "lnb")
PREP_SLAB = 2 * LANES


def _ones_blocks():
    ri = lax.broadcasted_iota(jnp.int32, (2 * LANES, 2 * LANES), 0)
    ci = lax.broadcasted_iota(jnp.int32, (2 * LANES, 2 * LANES), 1)
    return jnp.where((ri >> 6) == (ci >> 6), 1.0, 0.0).astype(BF16)


def _rwkv_prep(x, first_ref, w, seq, out, ps_ref, lora_ref):
    nseq = ROWS // seq
    h = _rms(x, w["g"][...])
    proj = _mm(h, w["wr"][...])
    ps_ref[...] = proj[:, :SHIFT_W]
    out["sgb"][...] = _silu(proj[:, SHIFT_W:])
    pos = lax.broadcasted_iota(jnp.int32, (ROWS, 1), 0) & (seq - 1)

    def shifted(cols):
        cur = ps_ref[:, cols]
        prev = jnp.where(pos == 0, first_ref[:, cols], pltpu.roll(cur, 1, 0))
        return cur + (prev - cur) * w["mu"][:, cols]

    zl = shifted(slice(3 * B_WIDTH, SHIFT_W))
    lo_half = lax.broadcasted_iota(jnp.int32, (1, LANES), 1) < HEAD_DIM
    lora_ref[...] = _mm(jnp.where(lo_half, jnp.tanh(zl), zl), w["lora"][...])
    yield
    ones_blocks = _ones_blocks()
    for d in range(B_WIDTH // PREP_SLAB):
        cols = slice(PREP_SLAB * d, PREP_SLAB * (d + 1))

        def wcols(name):
            return w[name][:, cols]

        r = shifted(cols)
        k = shifted(slice(B_WIDTH + cols.start, B_WIDTH + cols.stop))
        v = shifted(slice(2 * B_WIDTH + cols.start, 2 * B_WIDTH + cols.stop))
        lw = -math.exp(-0.5) * _sigmoid(wcols("w0") + lora_ref[:, cols])
        asig = _sigmoid(wcols("a0") + lora_ref[:, slice(B_WIDTH + cols.start, B_WIDTH + cols.stop)])
        kkr = k * wcols("kk")
        kk = kkr * lax.rsqrt(jnp.maximum(_head_sums(kkr * kkr, ones_blocks), 1e-24))
        kp = k * (1.0 + (asig - 1.0) * wcols("ka"))
        a_ = -kk
        b_ = kk * asig
        cum = _seg_cumsum(lw, seq)
        if nseq == 1:
            tot = jnp.broadcast_to(cum[ROWS - 1:ROWS, :], cum.shape)
        else:
            c3 = cum.reshape(nseq, seq, PREP_SLAB)
            tot = jnp.broadcast_to(c3[:, seq - 1:seq, :], c3.shape).reshape(cum.shape)
        g_inv = jnp.exp(-cum)
        g_last = jnp.exp(tot - cum)
        out["at"][:, cols] = a_ * jnp.exp(cum - lw)
        out["rt"][:, cols] = r * jnp.exp(cum)
        bt = b_ * g_inv
        kt = kp * g_inv
        bh = b_ * g_last
        kh = kp * g_last
        for q in range(PREP_SLAB // LANES):
            ql = slice(LANES * q, LANES * (q + 1))
            pair = PREP_SLAB // LANES * d + q
            out["bkt"][pair] = _bf(jnp.concatenate([bt[:, ql], kt[:, ql]], axis=0).T)
            out["bkh"][pair] = _bf(jnp.concatenate([bh[:, ql], kh[:, ql]], axis=0))
        out["gt"][:, cols] = jnp.exp(tot)
        out["v"][:, cols] = v
        out["bonus"][:, cols] = _head_sums(r * kp * wcols("rk"), ones_blocks) * v
        yield


def _rwkv_core(buf, w, seq, av_ref, t_ref, arbk_ref, res, *, s_ref=None, si_ref=None, so_ref=None,
               x0_ref=None, y0_ref=None, uvt_ref=None):
    nseq = ROWS // seq
    prompt = s_ref is not None
    lo_half = lax.broadcasted_iota(jnp.int32, (1, LANES), 1) < HEAD_DIM
    hi_half = jnp.logical_not(lo_half)
    rr = lax.broadcasted_iota(jnp.int32, (ROWS, ROWS), 0)
    cc = lax.broadcasted_iota(jnp.int32, (ROWS, ROWS), 1)
    shift = int(math.log2(seq))
    same = (rr >> shift) == (cc >> shift)
    strict = same & (rr > cc)
    incl = same & (rr >= cc)
    bd = (rr < HEAD_DIM) == (cc < HEAD_DIM)
    pair_slices = [slice(LANES * p, LANES * (p + 1)) for p in range(N_PAIRS)]
    at_ref, rt_ref, bkt_ref, v_ref, gt_ref, bkh_ref = (buf[n] for n in ("at", "rt", "bkt", "v", "gt", "bkh"))

    def halves(t):
        return jnp.concatenate([jnp.where(lo_half, t, 0.0), jnp.where(hi_half, t, 0.0)], axis=0)

    def blockdiag(s_nat):
        return jnp.concatenate([jnp.where(lo_half, s_nat, 0.0), jnp.where(lo_half, 0.0, s_nat)], axis=0)

    if prompt:
        s0s = [s_ref[p] for p in range(N_PAIRS)]
        s0ts = [_bf(s0.T) for s0 in s0s]
        x0s = [_mm(at_ref[:, sl], s0t) for sl, s0t in zip(pair_slices, s0ts)]
        y0s = [_mm(rt_ref[:, sl], s0t) for sl, s0t in zip(pair_slices, s0ts)]
    else:
        def read_body(j, carry):
            rows = pl.ds(pl.multiple_of(j * seq, seq), seq)
            for p, sl in enumerate(pair_slices):
                s0 = blockdiag(si_ref[j, p])
                arj = jnp.concatenate([at_ref[rows, sl], rt_ref[rows, sl]], axis=0)
                xy = _mm_nt(arj, s0)
                x0_ref[rows, sl] = xy[:seq]
                y0_ref[rows, sl] = xy[seq:]
            return carry
        lax.fori_loop(0, nseq, read_body, 0, unroll=SEQ_UNROLL)
        x0s = [x0_ref[:, sl] for sl in pair_slices]
        y0s = [y0_ref[:, sl] for sl in pair_slices]
    yield

    a_abs, a_aks = [], []
    for p, sl in enumerate(pair_slices):
        rbk = []
        for hh in range(2):
            half = lo_half if hh == 0 else hi_half
            ga = _mm(jnp.where(half, at_ref[:, sl], 0.0), bkt_ref[p])
            gr = _mm(jnp.where(half, rt_ref[:, sl], 0.0), bkt_ref[p])
            a_abs.append(jnp.where(strict, ga[:, :ROWS], 0.0))
            a_aks.append(jnp.where(strict, ga[:, ROWS:], 0.0))
            rbk.append(jnp.where(incl, gr[:, :ROWS], 0.0))
            rbk.append(jnp.where(incl, gr[:, ROWS:], 0.0))
        arbk_ref[p] = _bf(jnp.concatenate(rbk, axis=1))
        av_ref[:, sl] = _mm(jnp.concatenate(a_aks[2 * p:2 * p + 2], axis=1), halves(v_ref[:, sl]))
        if p % 4 == 3:
            yield
    ts = []
    yield from _neumann_levels(a_abs, seq, ts)
    for p in range(N_PAIRS):
        t_ref[p] = _bf(jnp.concatenate(ts[2 * p:2 * p + 2], axis=1))

    vss = [v_ref[:, sl] for sl in pair_slices]
    us = [jnp.dot(t_ref[p], _bf(halves(x0s[p] + av_ref[:, sl])), preferred_element_type=F32)
          for p, sl in enumerate(pair_slices)]
    yield
    ys = []
    for p in range(N_PAIRS):
        u, vs = us[p], vss[p]
        rhs = jnp.concatenate([jnp.where(lo_half, u, 0.0), jnp.where(lo_half, vs, 0.0),
                               jnp.where(hi_half, u, 0.0), jnp.where(hi_half, vs, 0.0)], axis=0)
        ys.append(y0s[p] + jnp.dot(arbk_ref[p], _bf(rhs), preferred_element_type=F32))
    yield

    uvts = [jnp.concatenate([us[p], vss[p]], axis=0).T for p in range(N_PAIRS)]
    if prompt:
        s1s = [s0s[p] * gt_ref[0:1, sl] + jnp.where(bd, _mm(uvts[p], bkh_ref[p]), 0.0)
               for p, sl in enumerate(pair_slices)]
        for p in range(N_PAIRS):
            s_ref[p] = s1s[p]
        res["s1"] = s1s
    else:
        colseq = (lax.broadcasted_iota(jnp.int32, (1, 2 * ROWS), 1) & (ROWS - 1)) >> shift
        for p in range(N_PAIRS):
            uvt_ref[p] = _bf(uvts[p])

        def upd_body(j, carry):
            row = pl.ds(pl.multiple_of(j * seq, seq), 1)
            for p, sl in enumerate(pair_slices):
                uvt = jnp.where(colseq == j, uvt_ref[p], jnp.zeros((), BF16))
                upd = jnp.dot(uvt, bkh_ref[p], preferred_element_type=F32)
                so_ref[j, p] = (si_ref[j, p] * gt_ref[row, sl]
                                + jnp.where(lo_half, upd[:HEAD_DIM], upd[HEAD_DIM:]))
            return carry
        lax.fori_loop(0, nseq, upd_body, 0, unroll=SEQ_UNROLL)
    yield

    ones_blocks = _ones_blocks()
    y = jnp.concatenate(ys, axis=1)
    mean = _head_sums(y, ones_blocks, terms=1) * (1.0 / HEAD_DIM)
    dlt = y - mean
    var = _head_sums(dlt * dlt, ones_blocks, terms=1) * (1.0 / HEAD_DIM)
    yn = dlt * lax.rsqrt(var + GN_EPS) * w["lng"][...] + w["lnb"][...]
    yo = (yn + buf["bonus"][...]) * buf["sgb"][...]
    res["yb"] = _mm(yo, w["wob"][...])


N_RWKV_W = len(RWKV_WEIGHT_NAMES)
N_PREP = len(PREP_NAMES)
PROMPT_ORDER = "CCPCCPCCPCCPCCPCCC"


def _rwkv_prompt_kernel(*refs, tiles_per_seq, n_tiles):
    x_ref = refs[0]
    w = dict(zip(RWKV_WEIGHT_NAMES, refs[1:1 + N_RWKV_W]))
    ye_ref, yo_ref, so_ref, sh_ref = refs[1 + N_RWKV_W:5 + N_RWKV_W]
    scratch = refs[5 + N_RWKV_W:]
    set_a = dict(zip(PREP_NAMES, scratch[:N_PREP]))
    set_b = dict(zip(PREP_NAMES, scratch[N_PREP:2 * N_PREP]))
    av_ref, lora_ref, t_ref, arbk_ref, s_ref, ps_ref, carry_ref = scratch[2 * N_PREP:]
    j = pl.program_id(0)
    starts_seq = lax.rem(2 * j, tiles_per_seq) == 0

    @pl.when(j == 0)
    def _():
        for name in PREP_NAMES:
            set_b[name][...] = jnp.zeros_like(set_b[name])
        s_ref[...] = jnp.zeros_like(s_ref)

    @pl.when(starts_seq)
    def _():
        carry_ref[...] = jnp.zeros_like(carry_ref)

    def half_step(x, prepared, current):
        res = {}
        prep = _rwkv_prep(x, carry_ref, w, ROWS, prepared, ps_ref, lora_ref)
        core = _rwkv_core(current, w, ROWS, av_ref, t_ref, arbk_ref, res, s_ref=s_ref)
        for kind in PROMPT_ORDER:
            next(prep if kind == "P" else core, None)
        for _ in prep:
            pass
        for _ in core:
            pass
        carry_ref[...] = ps_ref[ROWS - 1:ROWS, :]
        return res

    res = half_step(x_ref[0, 0:ROWS, :], set_a, set_b)
    yo_ref[0, 0] = res["yb"]

    @pl.when(starts_seq & (j > 0))
    def _():
        lo_half = lax.broadcasted_iota(jnp.int32, (1, LANES), 1) < HEAD_DIM
        for p in range(N_PAIRS):
            so_ref[0, p] = jnp.where(lo_half, res["s1"][p][:HEAD_DIM], res["s1"][p][HEAD_DIM:])

    @pl.when(starts_seq)
    def _():
        s_ref[...] = jnp.zeros_like(s_ref)

    res = half_step(x_ref[0, ROWS:2 * ROWS, :], set_b, set_a)
    sh_ref[0] = ps_ref[ROWS - 1:ROWS, :]

    @pl.when(2 * j < n_tiles)
    def _():
        ye_ref[0, 0] = res["yb"]


def _rwkv_sample_kernel(*refs, seq):
    x_ref = refs[0]
    w = dict(zip(RWKV_WEIGHT_NAMES, refs[1:1 + N_RWKV_W]))
    si_ref, shin_ref, yb_ref, so_ref, ps_ref = refs[1 + N_RWKV_W:6 + N_RWKV_W]
    scratch = refs[6 + N_RWKV_W:]
    buf = dict(zip(PREP_NAMES, scratch[:N_PREP]))
    av_ref, lora_ref, t_ref, arbk_ref, x0_ref, y0_ref, uvt_ref = scratch[N_PREP:]
    res = {}
    for _ in _rwkv_prep(x_ref[...], shin_ref, w, seq, buf, ps_ref, lora_ref):
        pass
    for _ in _rwkv_core(buf, w, seq, av_ref, t_ref, arbk_ref, res, si_ref=si_ref, so_ref=so_ref,
                        x0_ref=x0_ref, y0_ref=y0_ref, uvt_ref=uvt_ref):
        pass
    yb_ref[...] = res["yb"]


def _const_spec(shape):
    nd = len(shape)
    return pl.BlockSpec(shape, lambda *_: (0,) * nd, pipeline_mode=pl.Buffered(1))


def _smem_spec():
    return pl.BlockSpec(memory_space=pltpu.SMEM)


def _params(n_axes):
    return pltpu.CompilerParams(dimension_semantics=("arbitrary",) * n_axes, vmem_limit_bytes=VMEM_LIMIT)


def _attn_weight_specs():
    return [_const_spec((1, D_MODEL)), _const_spec((D_MODEL, ATTN_COLS + MERGE_COLS)),
            _const_spec((A_WIDTH, D_MODEL)), _const_spec((D_MODEL, D_MODEL)), _const_spec((1, D_MODEL)),
            _smem_spec(), _smem_spec()]


def _attention_prompt(x, yb_even, yb_odd, weights, final):
    bsz, t, _ = x.shape
    rows = ATTN_BLOCKS * WINDOW
    nb = t // rows
    tile = pl.BlockSpec((1, rows, D_MODEL), lambda b, n: (b, n, 0))
    half = pl.BlockSpec((1, ATTN_BLOCKS // 2, ROWS, D_MODEL), lambda b, n: (b, n, 0, 0))
    win = pl.BlockSpec((1, WINDOW, KV_WIDTH), lambda b, n: (b, 0, 0))
    return pl.pallas_call(
        functools.partial(_attn_prompt_kernel, final=final),
        grid=(bsz, nb),
        in_specs=[tile, half, half] + _attn_weight_specs(),
        out_specs=[tile, win, win],
        out_shape=[jax.ShapeDtypeStruct((bsz, t, D_MODEL), F32),
                   jax.ShapeDtypeStruct((bsz, WINDOW, KV_WIDTH), F32),
                   jax.ShapeDtypeStruct((bsz, WINDOW, KV_WIDTH), F32)],
        scratch_shapes=[pltpu.VMEM((WINDOW, KV_WIDTH), F32), pltpu.VMEM((WINDOW, KV_WIDTH), F32),
                        pltpu.VMEM((A_HEADS, WINDOW, WINDOW), F32), pltpu.VMEM((A_HEADS, WINDOW, WINDOW), F32)],
        compiler_params=_params(2),
        name="attn_prompt",
    )(x, yb_even, yb_odd, *weights)


def _attention_sample(x2d, yb, seq, weights, kc, vc, final):
    rows = x2d.shape[0]
    nseq = ROWS // seq
    tile = pl.BlockSpec((ROWS, D_MODEL), lambda i: (i, 0))
    cache = pl.BlockSpec((nseq, WINDOW, KV_WIDTH), lambda i: (i, 0, 0))
    return pl.pallas_call(
        functools.partial(_attn_sample_kernel, seq=seq, final=final),
        grid=(rows // ROWS,),
        in_specs=[tile, tile] + _attn_weight_specs() + [cache, cache],
        out_specs=[tile, cache, cache],
        out_shape=[jax.ShapeDtypeStruct((rows, D_MODEL), F32),
                   jax.ShapeDtypeStruct(kc.shape, F32), jax.ShapeDtypeStruct(vc.shape, F32)],
        scratch_shapes=[pltpu.VMEM((ROWS, A_WIDTH), F32), pltpu.VMEM((ROWS, KV_WIDTH), F32),
                        pltpu.VMEM((ROWS, KV_WIDTH), F32), pltpu.VMEM((ROWS, A_WIDTH), F32),
                        pltpu.VMEM((A_HEADS, WINDOW, WINDOW), F32), pltpu.VMEM((A_HEADS, WINDOW, WINDOW), F32)],
        compiler_params=_params(1),
        name="attn_sample",
    )(x2d, yb, *weights, kc, vc)


def _prep_scratch(names):
    shapes = {"bkt": pltpu.VMEM((N_PAIRS, LANES, 2 * ROWS), BF16), "bkh": pltpu.VMEM((N_PAIRS, 2 * ROWS, LANES), BF16)}
    return [shapes.get(n, pltpu.VMEM((ROWS, B_WIDTH), F32)) for n in names]


def _core_scratch():
    return [pltpu.VMEM((ROWS, B_WIDTH), F32), pltpu.VMEM((ROWS, 2 * B_WIDTH), F32),
            pltpu.VMEM((N_PAIRS, ROWS, 2 * ROWS), BF16), pltpu.VMEM((N_PAIRS, ROWS, 4 * ROWS), BF16)]


def _rwkv_weight_specs():
    row = _const_spec((1, B_WIDTH))
    return [_const_spec((1, D_MODEL)), _const_spec((D_MODEL, RWKV_COLS)), _const_spec((B_WIDTH, D_MODEL)),
            _const_spec((1, SHIFT_W)), row, _const_spec((LANES, 2 * B_WIDTH)), row, row, row, row, row, row]


def _rwkv_prompt(x, weights):
    bsz, t, _ = x.shape
    nb = t // ROWS
    npair = nb // 2
    total = bsz * npair

    def pair_of(jj):
        return lax.div(jj, npair), lax.rem(jj, npair)

    def x_map(j):
        return (*pair_of(jnp.minimum(j, total - 1)), 0)

    def even_map(j):
        return (*pair_of(jnp.minimum(j, total - 1)), 0, 0)

    def odd_map(j):
        return (*pair_of(jnp.maximum(j - 1, 0)), 0, 0)

    ytile = jax.ShapeDtypeStruct((bsz, npair, ROWS, D_MODEL), F32)
    return pl.pallas_call(
        functools.partial(_rwkv_prompt_kernel, tiles_per_seq=nb, n_tiles=bsz * nb),
        grid=(total + 1,),
        in_specs=[pl.BlockSpec((1, 2 * ROWS, D_MODEL), x_map)] + _rwkv_weight_specs(),
        out_specs=[pl.BlockSpec((1, 1, ROWS, D_MODEL), even_map),
                   pl.BlockSpec((1, 1, ROWS, D_MODEL), odd_map),
                   pl.BlockSpec((1, N_PAIRS, HEAD_DIM, LANES), lambda j: (odd_map(j)[0], 0, 0, 0)),
                   pl.BlockSpec((1, 1, SHIFT_W), lambda j: (x_map(j)[0], 0, 0))],
        out_shape=[ytile, ytile,
                   jax.ShapeDtypeStruct((bsz, N_PAIRS, HEAD_DIM, LANES), F32),
                   jax.ShapeDtypeStruct((bsz, 1, SHIFT_W), F32)],
        scratch_shapes=_prep_scratch(PREP_NAMES) + _prep_scratch(PREP_NAMES) + _core_scratch()
        + [pltpu.VMEM((N_PAIRS, LANES, LANES), F32), pltpu.VMEM((ROWS, SHIFT_W), F32),
           pltpu.VMEM((1, SHIFT_W), F32)],
        compiler_params=_params(1),
        name="rwkv_prompt",
    )(x, *weights)


def _rwkv_sample(x2d, seq, weights, s_nat, shift_rows):
    rows = x2d.shape[0]
    nseq = ROWS // seq
    tile = pl.BlockSpec((ROWS, D_MODEL), lambda i: (i, 0))
    state = pl.BlockSpec((nseq, N_PAIRS, HEAD_DIM, LANES), lambda i: (i, 0, 0, 0))
    shift = pl.BlockSpec((ROWS, SHIFT_W), lambda i: (i, 0))
    wide = pltpu.VMEM((ROWS, B_WIDTH), F32)
    return pl.pallas_call(
        functools.partial(_rwkv_sample_kernel, seq=seq),
        grid=(rows // ROWS,),
        in_specs=[tile] + _rwkv_weight_specs() + [state, shift],
        out_specs=[tile, state, shift],
        out_shape=[jax.ShapeDtypeStruct((rows, D_MODEL), F32),
                   jax.ShapeDtypeStruct(s_nat.shape, F32),
                   jax.ShapeDtypeStruct((rows, SHIFT_W), F32)],
        scratch_shapes=_prep_scratch(PREP_NAMES) + _core_scratch()
        + [wide, wide, pltpu.VMEM((N_PAIRS, ROWS, 2 * ROWS), BF16)],
        compiler_params=_params(1),
        name="rwkv_sample",
    )(x2d, *weights, s_nat, shift_rows)


def _pairs_from_heads(s):
    b = s.shape[0]
    return s.reshape(b, N_PAIRS, 2, HEAD_DIM, HEAD_DIM).transpose(0, 1, 3, 2, 4).reshape(b, N_PAIRS, HEAD_DIM, LANES)


def _heads_from_pairs(s):
    b = s.shape[0]
    return s.reshape(b, N_PAIRS, HEAD_DIM, 2, HEAD_DIM).transpose(0, 1, 3, 2, 4).reshape(b, B_HEADS, HEAD_DIM, HEAD_DIM)


def kernel(x_prompt, x_sample, cache_k_win, cache_v_win, state_wkv, state_shift, rel_bias, norm_g, w_in, attn_sinks, shift_mu, rwkv_w0, rwkv_w2, rwkv_a0, rwkv_a2, rwkv_k_k, rwkv_k_a, rwkv_r_k, lnx_g, lnx_b, w_out_a, w_out_b, w_o, final_g):
    depth = w_in.shape[0]
    bsz, t, _ = x_prompt.shape
    dbsz, dseq, _ = x_sample.shape
    hp = x_prompt
    hs = x_sample.reshape(dbsz * dseq, D_MODEL)
    fg = final_g.reshape(1, D_MODEL)
    outs = [[] for _ in range(8)]
    for l in range(depth):
        g = norm_g[l].reshape(1, D_MODEL)
        w = w_in[l]
        wa = _bf(jnp.concatenate([w[:, :ATTN_COLS], w[:, ATTN_COLS + RWKV_COLS:]], axis=1))
        wr = _bf(w[:, ATTN_COLS:ATTN_COLS + RWKV_COLS])
        woa, wob, wo = _bf(w_out_a[l]), _bf(w_out_b[l]), _bf(w_o[l])
        zeros = jnp.zeros((DECAY_LORA, B_WIDTH), F32)
        lora = _bf(jnp.concatenate([jnp.concatenate([rwkv_w2[l], zeros], axis=1),
                                    jnp.concatenate([zeros, rwkv_a2[l]], axis=1)], axis=0))
        rw = [g, wr, wob, shift_mu[l].reshape(1, SHIFT_W), rwkv_w0[l].reshape(1, B_WIDTH), lora,
              rwkv_a0[l].reshape(1, B_WIDTH), rwkv_k_k[l].reshape(1, B_WIDTH), rwkv_k_a[l].reshape(1, B_WIDTH),
              rwkv_r_k[l].reshape(1, B_WIDTH), lnx_g[l].reshape(1, B_WIDTH), lnx_b[l].reshape(1, B_WIDTH)]
        aw = [g, wa, woa, wo, fg, rel_bias, attn_sinks[l]]
        final = l == depth - 1

        yb_even, yb_odd, s1, t1 = _rwkv_prompt(hp, rw)
        hp, k1, v1 = _attention_prompt(hp, yb_even, yb_odd, aw, final)

        kc = cache_k_win[l].reshape(dbsz, WINDOW, KV_WIDTH)
        vc = cache_v_win[l].reshape(dbsz, WINDOW, KV_WIDTH)
        shift_rows = jnp.repeat(state_shift[l], dseq, axis=0)
        yb_s, s2, ps_s = _rwkv_sample(hs, dseq, rw, _pairs_from_heads(state_wkv[l]), shift_rows)
        hs, k2, v2 = _attention_sample(hs, yb_s, dseq, aw, kc, vc, final)

        outs[0].append(k1.reshape(bsz, WINDOW, A_KV_HEADS, HEAD_DIM))
        outs[1].append(v1.reshape(bsz, WINDOW, A_KV_HEADS, HEAD_DIM))
        outs[2].append(_heads_from_pairs(s1))
        outs[3].append(t1.reshape(bsz, SHIFT_W))
        outs[4].append(k2.reshape(dbsz, WINDOW, A_KV_HEADS, HEAD_DIM))
        outs[5].append(v2.reshape(dbsz, WINDOW, A_KV_HEADS, HEAD_DIM))
        outs[6].append(_heads_from_pairs(s2))
        outs[7].append(ps_s.reshape(dbsz, dseq, SHIFT_W)[:, -1])
    y_prompt = hp
    y_sample = hs.reshape(dbsz, dseq, D_MODEL)
    return (y_prompt, y_sample) + tuple(jnp.stack(o) for o in outs)
```
